```python
import jax, jax.numpy as jnp
from jax import lax
import numpy as np

D_MODEL = 1024
BATCH = 4
SEQ = 8192
DEPTH = 1

CHUNK = 64
N_META = 16

A_HEADS = 8
A_HEAD_DIM = 64
A_WIDTH = A_HEADS * A_HEAD_DIM
Q_RANK = 256
KV_RANK = 128
IDX_HEADS = 8
IDX_DIM = 64
INDEX_TOPK = 256
Q_BLOCK = 128

B_HEADS = 8
B_HEAD_DIM = 64
B_WIDTH = B_HEADS * B_HEAD_DIM
DECAY_LORA = 64
AAA_LORA = 64
GATE_LORA = 128
GN_EPS = 64e-5

D_FF = 2816
CONV_W = 3

LN_EPS = 1e-5
ALPHA = (2 * DEPTH) ** 0.25
BETA = (8 * DEPTH) ** -0.25

A_COLS = Q_RANK + KV_RANK + IDX_DIM + IDX_HEADS
B_COLS = 3 * B_WIDTH + DECAY_LORA + AAA_LORA + GATE_LORA
IN_COLS = A_COLS + B_COLS

kernel_name = "hymba_dsa_rwkv7_convffn_deepnorm"


def layer_norm(x, g, b, eps=LN_EPS):
    xf = x.astype(jnp.float32)
    mu = jnp.mean(xf, -1, keepdims=True)
    var = jnp.mean(jnp.square(xf - mu), -1, keepdims=True)
    return ((xf - mu) * lax.rsqrt(var + eps) * g + b).astype(x.dtype)


def rms_norm(x, g, eps=1e-6):
    xf = x.astype(jnp.float32)
    y = xf * lax.rsqrt(jnp.mean(jnp.square(xf), -1, keepdims=True) + eps) * g
    return y.astype(x.dtype)


def chunk_ids(pos):
    return (pos - N_META) // CHUNK + 1


def alibi_slopes(n_heads):
    return jnp.asarray([2.0 ** (-8.0 * (h + 1) / n_heads) for h in range(n_heads)], jnp.float32)


def sparse_indexed_attention(pa, q_norm_g, w_uq, kv_norm_g, w_ukv, w_iq, idx_ln_g, idx_ln_b):
    bsz, T, _ = pa.shape
    c_q = pa[..., :Q_RANK]
    c_kv = pa[..., Q_RANK:Q_RANK + KV_RANK]
    k_idx_raw = pa[..., Q_RANK + KV_RANK:Q_RANK + KV_RANK + IDX_DIM]
    w_idx = pa[..., Q_RANK + KV_RANK + IDX_DIM:]
    c_q = rms_norm(c_q, q_norm_g)
    q = (c_q @ w_uq).reshape(bsz, T, A_HEADS, A_HEAD_DIM)
    q_idx = (c_q @ w_iq).reshape(bsz, T, IDX_HEADS, IDX_DIM)
    kv = (rms_norm(c_kv, kv_norm_g) @ w_ukv).reshape(bsz, T, A_HEADS, 2 * A_HEAD_DIM)
    k, v = kv[..., :A_HEAD_DIM], kv[..., A_HEAD_DIM:]
    k_idx = layer_norm(k_idx_raw, idx_ln_g, idx_ln_b)
    w_idx = w_idx * (IDX_HEADS ** -0.5 * IDX_DIM ** -0.5)

    topk = min(INDEX_TOPK, T // 4)
    n_blk = -(-T // Q_BLOCK)
    t_pad = n_blk * Q_BLOCK

    def to_blocks(a):
        a = jnp.pad(a, [(0, 0), (0, t_pad - T)] + [(0, 0)] * (a.ndim - 2))
        return jnp.moveaxis(a.reshape((bsz, n_blk, Q_BLOCK) + a.shape[2:]), 1, 0)

    key_pos = jnp.arange(T, dtype=jnp.int32)
    key_chunk = chunk_ids(key_pos)
    slopes = alibi_slopes(A_HEADS)
    scale = A_HEAD_DIM ** -0.5

    def block(args):
        q_b, qi_b, wi_b, pos_b = args
        qc = chunk_ids(pos_b)
        visible = key_chunk[None, :] <= qc[:, None]
        rel = jax.nn.relu(jnp.einsum('bqhd,bsd->bqhs', qi_b, k_idx,
                                     preferred_element_type=jnp.float32))
        score = jnp.einsum('bqhs,bqh->bqs', rel, wi_b.astype(jnp.float32))
        score = jnp.where(visible[None], score, -jnp.inf)
        _, sel = lax.top_k(score, topk)
        k_sel = jax.vmap(lambda kb, ib: kb[ib])(k, sel)
        v_sel = jax.vmap(lambda vb, ib: vb[ib])(v, sel)
        sel_ok = key_chunk[sel] <= qc[None, :, None]
        dist = jnp.abs(pos_b[None, :, None] - sel).astype(jnp.float32)
        logits = jnp.einsum('bqhd,bqkhd->bqhk', q_b, k_sel,
                            preferred_element_type=jnp.float32) * scale
        logits = logits - slopes[None, None, :, None] * dist[:, :, None, :]
        logits = jnp.where(sel_ok[:, :, None, :], logits, -jnp.inf)
        p = jax.nn.softmax(logits, axis=-1)
        return jnp.einsum('bqhk,bqkhd->bqhd', p.astype(v_sel.dtype), v_sel)

    pos_blocks = jnp.arange(t_pad, dtype=jnp.int32).reshape(n_blk, Q_BLOCK)
    out = lax.map(block, (to_blocks(q), to_blocks(q_idx), to_blocks(w_idx), pos_blocks))
    out = jnp.moveaxis(out, 0, 1).reshape(bsz, t_pad, A_WIDTH)[:, :T]
    return out.astype(pa.dtype)


def token_shift(p):
    return jnp.pad(p, [(0, 0), (1, 0), (0, 0)])[:, :-1]


def rwkv7_time_mix(pb, mu_shift, w0, w_decay_up, a0, w_aaa_up, w_gate_up, k_k, k_a, r_k, gn_g, gn_b):
    bsz, T, _ = pb.shape
    pb = pb + (token_shift(pb) - pb) * mu_shift
    s1, s2, s3 = B_WIDTH, 2 * B_WIDTH, 3 * B_WIDTH
    s4, s5 = s3 + DECAY_LORA, s3 + DECAY_LORA + AAA_LORA
    r, k, v = pb[..., :s1], pb[..., s1:s2], pb[..., s2:s3]
    wd, ad, gd = pb[..., s3:s4], pb[..., s4:s5], pb[..., s5:]
    w_log = -jax.nn.softplus(-(w0 + jnp.tanh(wd) @ w_decay_up)) - 0.5
    decay = jnp.exp(-jnp.exp(w_log.astype(jnp.float32)))
    a = jax.nn.sigmoid(a0 + ad @ w_aaa_up)
    g = jax.nn.sigmoid(gd) @ w_gate_up

    def heads(t):
        return t.astype(jnp.float32).reshape(bsz, T, B_HEADS, B_HEAD_DIM)

    kk = heads(k * k_k)
    kk = kk / jnp.maximum(jnp.sqrt(jnp.sum(jnp.square(kk), -1, keepdims=True)), 1e-12)
    k = k * (1 + (a - 1) * k_a)
    r_h, k_h, v_h, a_h, w_h = heads(r), heads(k), heads(v), heads(a), heads(decay)
    a_vec = -kk
    b_vec = kk * a_h

    def step(S, inp):
        r_t, w_t, k_t, v_t, av_t, bv_t = inp
        sa = jnp.einsum('bhvk,bhk->bhv', S, av_t)
        S = S * w_t[:, :, None, :] + sa[..., None] * bv_t[:, :, None, :] + v_t[..., None] * k_t[:, :, None, :]
        return S, jnp.einsum('bhvk,bhk->bhv', S, r_t)

    xs = tuple(jnp.moveaxis(t, 1, 0) for t in (r_h, w_h, k_h, v_h, a_vec, b_vec))
    S0 = jnp.zeros((bsz, B_HEADS, B_HEAD_DIM, B_HEAD_DIM), jnp.float32)
    _, y = lax.scan(step, S0, xs)
    y = jnp.moveaxis(y, 0, 1)
    mu = jnp.mean(y, -1, keepdims=True)
    var = jnp.mean(jnp.square(y - mu), -1, keepdims=True)
    y = ((y - mu) * lax.rsqrt(var + GN_EPS)).reshape(bsz, T, B_WIDTH) * gn_g + gn_b
    bonus = jnp.sum(r_h * k_h * r_k, -1, keepdims=True) * v_h
    y = (y + bonus.reshape(bsz, T, B_WIDTH)) * g
    return y.astype(pb.dtype)


def conv_ffn(x, w_ffn_up, conv_w, conv_b, w_ffn_down):
    up = x @ w_ffn_up
    gate, val = up[..., :D_FF], up[..., D_FF:]
    gate = lax.conv_general_dilated(gate, conv_w[:, None, :], window_strides=(1,),
                                    padding=[(CONV_W - 1, 0)],
                                    dimension_numbers=('NWC', 'WIO', 'NWC'),
                                    feature_group_count=D_FF) + conv_b
    return (jax.nn.gelu(gate) * val) @ w_ffn_down


def setup_inputs(seed: int = 0) -> dict:
    key = jax.random.key(seed)
    ks = jax.random.split(key, 32)
    f32 = jnp.float32

    def nrm(k, shape, scale):
        return jax.random.normal(k, shape, f32) * scale

    def gain(k, shape):
        return 1.0 + 0.02 * jax.random.normal(k, shape, f32)

    def bias(k, shape):
        return 0.02 * jax.random.normal(k, shape, f32)

    L = DEPTH
    return {
        'x': nrm(ks[0], (BATCH, SEQ, D_MODEL), 1.0),
        'meta': nrm(ks[1], (N_META, D_MODEL), 1.0),
        'ln_in_g': gain(ks[2], (D_MODEL,)),
        'ln_in_b': bias(ks[3], (D_MODEL,)),
        'w_in': nrm(ks[4], (L, D_MODEL, IN_COLS), D_MODEL ** -0.5),
        'q_norm_g': gain(ks[5], (L, Q_RANK)),
        'w_uq': nrm(ks[6], (L, Q_RANK, A_WIDTH), Q_RANK ** -0.5),
        'kv_norm_g': gain(ks[7], (L, KV_RANK)),
        'w_ukv': nrm(ks[8], (L, KV_RANK, 2 * A_WIDTH), KV_RANK ** -0.5),
        'w_iq': nrm(ks[9], (L, Q_RANK, IDX_HEADS * IDX_DIM), Q_RANK ** -0.5),
        'idx_ln_g': gain(ks[10], (L, IDX_DIM)),
        'idx_ln_b': bias(ks[11], (L, IDX_DIM)),
        'mu_shift': jax.random.uniform(ks[12], (L, B_COLS), f32),
        'w0': jax.random.uniform(ks[13], (L, B_WIDTH), f32, -6.0, -1.0),
        'w_decay_up': nrm(ks[14], (L, DECAY_LORA, B_WIDTH), 0.1 * DECAY_LORA ** -0.5),
        'a0': nrm(ks[15], (L, B_WIDTH), 0.1),
        'w_aaa_up': nrm(ks[16], (L, AAA_LORA, B_WIDTH), 0.1 * AAA_LORA ** -0.5),
        'w_gate_up': nrm(ks[17], (L, GATE_LORA, B_WIDTH), GATE_LORA ** -0.5),
        'k_k': 0.85 + 0.02 * jax.random.normal(ks[18], (L, B_WIDTH), f32),
        'k_a': gain(ks[19], (L, B_WIDTH)),
        'r_k': nrm(ks[20], (L, B_HEADS, B_HEAD_DIM), 0.1),
        'gn_g': gain(ks[21], (L, B_WIDTH)),
        'gn_b': bias(ks[22], (L, B_WIDTH)),
        'w_out': nrm(ks[23], (L, A_WIDTH + B_WIDTH, D_MODEL), BETA * (A_WIDTH + B_WIDTH) ** -0.5),
        'ln1_g': gain(ks[24], (L, D_MODEL)),
        'ln1_b': bias(ks[25], (L, D_MODEL)),
        'w_ffn_up': nrm(ks[26], (L, D_MODEL, 2 * D_FF), D_MODEL ** -0.5),
        'conv_w': nrm(ks[27], (L, CONV_W, D_FF), CONV_W ** -0.5),
        'conv_b': bias(ks[28], (L, D_FF)),
        'w_ffn_down': nrm(ks[29], (L, D_FF, D_MODEL), BETA * D_FF ** -0.5),
        'ln2_g': gain(ks[30], (L, D_MODEL)),
        'ln2_b': bias(ks[31], (L, D_MODEL)),
    }


def reference(x, meta, ln_in_g, ln_in_b, w_in, q_norm_g, w_uq, kv_norm_g, w_ukv, w_iq,
              idx_ln_g, idx_ln_b, mu_shift, w0, w_decay_up, a0, w_aaa_up, w_gate_up,
              k_k, k_a, r_k, gn_g, gn_b, w_out, ln1_g, ln1_b, w_ffn_up, conv_w, conv_b,
              w_ffn_down, ln2_g, ln2_b):
    bsz = x.shape[0]
    meta_b = jnp.broadcast_to(meta[None].astype(x.dtype), (bsz, N_META, x.shape[-1]))
    h = jnp.concatenate([meta_b, x], axis=1)
    h = layer_norm(h, ln_in_g, ln_in_b)
    for l in range(DEPTH):
        p = h @ w_in[l]
        ya = sparse_indexed_attention(p[..., :A_COLS], q_norm_g[l], w_uq[l], kv_norm_g[l],
                                      w_ukv[l], w_iq[l], idx_ln_g[l], idx_ln_b[l])
        yb = rwkv7_time_mix(p[..., A_COLS:], mu_shift[l], w0[l], w_decay_up[l], a0[l],
                            w_aaa_up[l], w_gate_up[l], k_k[l], k_a[l], r_k[l], gn_g[l], gn_b[l])
        mix = jnp.concatenate([ya, yb], axis=-1) @ w_out[l]
        h = layer_norm(ALPHA * h + mix, ln1_g[l], ln1_b[l])
        ffn = conv_ffn(h, w_ffn_up[l], conv_w[l], conv_b[l], w_ffn_down[l])
        h = layer_norm(ALPHA * h + ffn, ln2_g[l], ln2_b[l])
    return h[:, N_META:]
```

```python
import functools

import jax
import jax.numpy as jnp
from jax import lax
from jax.experimental import pallas as pl
from jax.experimental.pallas import tpu as pltpu

F32 = jnp.float32
BF16 = jnp.bfloat16

D_MODEL = 1024
N_META = 16
CHUNK = 64
A_HEADS = 8
A_HEAD_DIM = 64
A_WIDTH = A_HEADS * A_HEAD_DIM
Q_RANK = 256
KV_RANK = 128
IDX_HEADS = 8
IDX_DIM = 64
INDEX_TOPK = 256
B_HEADS = 8
B_HEAD_DIM = 64
B_WIDTH = B_HEADS * B_HEAD_DIM
DECAY_LORA = 64
AAA_LORA = 64
GATE_LORA = 128
GN_EPS = 64e-5
D_FF = 2816
LN_EPS = 1e-5
DEPTH = 1
ALPHA = (2 * DEPTH) ** 0.25
A_COLS = Q_RANK + KV_RANK + IDX_DIM + IDX_HEADS
B_COLS = 3 * B_WIDTH + DECAY_LORA + AAA_LORA + GATE_LORA
A_PAD = 640
IN_PAD = A_PAD + B_COLS

LANES = 128
TQ = 256
OFF = TQ
PAD = OFF - N_META
FF_CHUNK = 256
N_FF_CHUNKS = D_FF // FF_CHUNK
GROUP = 256
NEG = -1e30
VMEM_LIMIT = 56 * 1024 * 1024


def _row_tile(tp):
    for cand in (768, 512, 256):
        if tp % cand == 0:
            return cand
    raise ValueError(tp)


def _layer_norm(x, g, b):
    mu = jnp.mean(x, -1, keepdims=True)
    xc = x - mu
    var = jnp.mean(xc * xc, -1, keepdims=True)
    return xc * lax.rsqrt(var + LN_EPS) * g + b


def _dot(a, b):
    return jnp.dot(a, b, preferred_element_type=F32)


def _dot_nt(a, b):
    return lax.dot_general(a, b, (((1,), (1,)), ((), ())), preferred_element_type=F32)


def _dot_tn(a, b):
    return lax.dot_general(a, b, (((0,), (0,)), ((), ())), preferred_element_type=F32)


def _const_spec(shape):
    nd = len(shape)
    return pl.BlockSpec(shape, lambda *_: (0,) * nd)


def _inproj_kernel(x_ref, lng_ref, lnb_ref, win_ref, qg_ref, wuq_ref, wiq_ref, kvg_ref, wuk_ref,
                   wuv_ref, ig_ref, ib_ref, mu_ref,
                   q_ref, qi_ref, wi_ref, k_ref, v_ref, kx_ref, pb_ref, carry_ref, *, tr):
    j = pl.program_id(1)

    @pl.when(j == 0)
    def _():
        carry_ref[...] = jnp.zeros_like(carry_ref)

    h = _layer_norm(x_ref[0], lng_ref[...], lnb_ref[...])
    pos = j * tr + lax.broadcasted_iota(jnp.int32, (tr, 1), 0)
    valid = pos >= PAD
    h = jnp.where(valid, h, 0.0)
    p = _dot(h.astype(BF16), win_ref[...])

    cq = p[:, :Q_RANK]
    cq = cq * lax.rsqrt(jnp.mean(cq * cq, -1, keepdims=True) + 1e-6) * qg_ref[...]
    cqb = cq.astype(BF16)
    q_ref[0] = (_dot(cqb, wuq_ref[...]) * (A_HEAD_DIM ** -0.5)).astype(BF16)
    qi_ref[0] = _dot(cqb, wiq_ref[...]).astype(BF16)
    ckv = p[:, Q_RANK:Q_RANK + KV_RANK]
    ckv = ckv * lax.rsqrt(jnp.mean(ckv * ckv, -1, keepdims=True) + 1e-6) * kvg_ref[...]
    ckvb = ckv.astype(BF16)
    k_ref[0] = _dot(ckvb, wuk_ref[...]).astype(BF16)
    v_ref[0] = _dot(ckvb, wuv_ref[...]).astype(BF16)
    kx_ref[0] = _layer_norm(p[:, 384:512], ig_ref[...], ib_ref[...]).astype(BF16)
    wi_ref[0] = p[:, 512:512 + IDX_HEADS] * (IDX_HEADS ** -0.5 * IDX_DIM ** -0.5)

    pb = p[:, A_PAD:]
    prev = carry_ref[7:8, :]
    row = lax.broadcasted_iota(jnp.int32, (tr, 1), 0)
    shifted = jnp.where(row == 0, prev, pltpu.roll(pb, 1, 0))
    carry_ref[...] = pb[tr - 8:, :]
    pbs = pb + (shifted - pb) * mu_ref[...]
    pb_ref[0] = jnp.where(valid, pbs, 0.0)


def _inproj(xcat, ln_g, ln_b, win, qg, wuq, wiq, kvg, wuk, wuv, ig2, ib2, mu):
    bsz, tp, _ = xcat.shape
    tr = _row_tile(tp)
    tok = lambda w: pl.BlockSpec((1, tr, w), lambda b, j: (b, j, 0))
    consts = [ln_g, ln_b, win, qg, wuq, wiq, kvg, wuk, wuv, ig2, ib2, mu]
    return pl.pallas_call(
        functools.partial(_inproj_kernel, tr=tr),
        grid=(bsz, tp // tr),
        in_specs=[tok(D_MODEL)] + [_const_spec(c.shape) for c in consts],
        out_specs=[tok(A_WIDTH), tok(A_WIDTH), tok(IDX_HEADS), tok(A_WIDTH), tok(A_WIDTH),
                   tok(LANES), tok(B_COLS)],
        out_shape=[jax.ShapeDtypeStruct((bsz, tp, A_WIDTH), BF16),
                   jax.ShapeDtypeStruct((bsz, tp, A_WIDTH), BF16),
                   jax.ShapeDtypeStruct((bsz, tp, IDX_HEADS), F32),
                   jax.ShapeDtypeStruct((bsz, tp, A_WIDTH), BF16),
                   jax.ShapeDtypeStruct((bsz, tp, A_WIDTH), BF16),
                   jax.ShapeDtypeStruct((bsz, tp, LANES), BF16),
                   jax.ShapeDtypeStruct((bsz, tp, B_COLS), F32)],
        scratch_shapes=[pltpu.VMEM((8, B_COLS), F32)],
        compiler_params=pltpu.CompilerParams(
            dimension_semantics=("arbitrary", "arbitrary"), vmem_limit_bytes=VMEM_LIMIT),
        name="inproj",
    )(xcat, *consts)


def _chunk_of(pos):
    return jnp.maximum((pos - OFF + CHUNK) >> 6, 0)


def _pair_blockdiag(slab, lo_half):
    zero = jnp.zeros_like(slab)
    return jnp.concatenate([jnp.where(lo_half, slab, zero), jnp.where(lo_half, zero, slab)], axis=0)


def _attn_kernel(q_ref, qi_ref, w_ref, kx_ref, k_ref, v_ref, o_ref,
                 s_ref, m_ref, l_ref, acc_ref, st_ref, bias_ref, tcar_ref, *, topk):
    i = pl.program_id(1)
    nkt = i + 1
    row = lax.broadcasted_iota(jnp.int32, (TQ, 1), 0)
    lane = lax.broadcasted_iota(jnp.int32, (1, TQ), 1)
    lo_half = lax.broadcasted_iota(jnp.int32, (TQ, LANES), 1) < A_HEAD_DIM
    qp = i * TQ + row
    qc = _chunk_of(qp)
    kf = float(topk)

    qi = qi_ref[0]
    w = w_ref[0]

    def score_body(kt, carry):
        rmax, rmin = carry
        base = pl.multiple_of(kt * TQ, TQ)
        z_idx = _pair_blockdiag(kx_ref[0, pl.ds(base, TQ), :], lo_half)
        acc = jnp.zeros((TQ, TQ), F32)
        for jp in range(IDX_HEADS // 2):
            zz = _dot_nt(qi[:, jp * LANES:(jp + 1) * LANES], z_idx)
            r = jnp.maximum(zz, 0.0)
            acc = acc + r[:, :TQ] * w[:, 2 * jp:2 * jp + 1] + r[:, TQ:] * w[:, 2 * jp + 1:2 * jp + 2]
        kp = base + lane
        kc = jnp.where(kp >= PAD, _chunk_of(kp), jnp.int32(1 << 30))
        vis = kc <= qc
        s_ref[:, pl.ds(base, TQ)] = jnp.where(vis, acc, -jnp.inf)
        rmax = jnp.maximum(rmax, jnp.max(jnp.where(vis, acc, -jnp.inf), axis=1, keepdims=True))
        rmin = jnp.minimum(rmin, jnp.min(jnp.where(vis, acc, jnp.inf), axis=1, keepdims=True))
        return rmax, rmin

    rmax, rmin = lax.fori_loop(0, nkt, score_body,
                               (jnp.full((TQ, 1), -jnp.inf, F32), jnp.full((TQ, 1), jnp.inf, F32)))

    nvis = (N_META + CHUNK * qc).astype(F32)
    all_visible = nvis <= kf
    zero = jnp.zeros((TQ, 1), F32)
    st_ref[:, 0:1] = rmin - jnp.maximum(jnp.abs(rmin) * (2.0 ** -20), 1e-30)
    st_ref[:, 1:2] = rmax
    st_ref[:, 2:3] = zero
    st_ref[:, 3:4] = jnp.full((TQ, 1), -jnp.inf, F32)
    st_ref[:, 4:5] = zero
    st_ref[:, 5:6] = jnp.where(all_visible, 1.0, 0.0)

    def count_above(thr):
        thr_b = jnp.broadcast_to(thr, (TQ, LANES))

        def cnt_body(kt, part):
            base = pl.multiple_of(kt * TQ, TQ)
            s = s_ref[:, pl.ds(base, TQ)]
            for c in range(TQ // LANES):
                part = part + jnp.where(s[:, c * LANES:(c + 1) * LANES] > thr_b, 1.0, 0.0)
            return part

        part = lax.fori_loop(0, nkt, cnt_body, jnp.zeros((TQ, LANES), F32))
        return jnp.sum(part, axis=1, keepdims=True)

    def search_cond(c):
        it, active = c
        return jnp.logical_and(active > 0.0, it < 160)

    def search_body(c):
        it, _ = c
        lo = st_ref[:, 0:1]
        hi = st_ref[:, 1:2]
        chi = st_ref[:, 2:3]
        thr = st_ref[:, 3:4]
        need = st_ref[:, 4:5]
        done = st_ref[:, 5:6]
        mid = lo + (hi - lo) * 0.5
        inside = jnp.logical_and(mid > lo, mid < hi)
        cnt = count_above(mid)
        nd = done < 0.5
        nd_in = jnp.logical_and(nd, inside)
        exact = jnp.logical_and(nd_in, cnt == kf)
        coll = jnp.logical_and(nd, jnp.logical_not(inside))
        go_lo = jnp.logical_and(nd_in, cnt > kf)
        go_hi = jnp.logical_and(nd_in, cnt < kf)
        st_ref[:, 3:4] = jnp.where(exact, mid, jnp.where(coll, hi, thr))
        st_ref[:, 4:5] = jnp.where(coll, kf - chi, need)
        done = jnp.where(jnp.logical_or(exact, coll), 1.0, done)
        st_ref[:, 5:6] = done
        st_ref[:, 0:1] = jnp.where(go_lo, mid, lo)
        st_ref[:, 1:2] = jnp.where(go_hi, mid, hi)
        st_ref[:, 2:3] = jnp.where(go_hi, cnt, chi)
        return it + 1, jnp.max(1.0 - done)

    lax.while_loop(search_cond, search_body, (jnp.int32(0), jnp.max(1.0 - st_ref[:, 5:6])))
    unfinished = st_ref[:, 5:6] < 0.5
    thr = jnp.where(unfinished, st_ref[:, 1:2], st_ref[:, 3:4])
    need = jnp.where(unfinished, kf - st_ref[:, 2:3], st_ref[:, 4:5])
    tie_any = jnp.max(need) > 0.0

    m_ref[...] = jnp.full(m_ref.shape, NEG, F32)
    l_ref[...] = jnp.zeros_like(l_ref)
    acc_ref[...] = jnp.zeros_like(acc_ref)
    tcar_ref[...] = jnp.zeros_like(tcar_ref)
    q = q_ref[0]

    def att_body(kt, _):
        base = pl.multiple_of(kt * TQ, TQ)
        s = s_ref[:, pl.ds(base, TQ)]
        bias_ref[...] = jnp.where(s > thr, 0.0, NEG)

        @pl.when(tie_any)
        def _():
            tie = jnp.logical_and(s == thr, need > 0.0)
            tie_b = jnp.where(tie, 1.0, 0.0).astype(BF16)
            upper = (lax.broadcasted_iota(jnp.int32, (TQ, TQ), 0)
                     <= lax.broadcasted_iota(jnp.int32, (TQ, TQ), 1))
            prefix = _dot(tie_b, jnp.where(upper, 1.0, 0.0).astype(BF16))
            rank = tcar_ref[...] + prefix
            take = jnp.logical_and(tie, rank <= need)
            bias_ref[...] = jnp.where(jnp.logical_or(s > thr, take), 0.0, NEG)
            tcar_ref[...] = tcar_ref[...] + prefix[:, TQ - 1:TQ]

        bias = bias_ref[...]
        kp = base + lane
        dist = jnp.abs(qp - kp).astype(F32)
        kk = k_ref[0, pl.ds(base, TQ), :]
        vv = v_ref[0, pl.ds(base, TQ), :]
        for jp in range(A_HEADS // 2):
            zk = _pair_blockdiag(kk[:, jp * LANES:(jp + 1) * LANES], lo_half)
            lg = _dot_nt(q[:, jp * LANES:(jp + 1) * LANES], zk)
            ps, alphas = [], []
            for hh in range(2):
                h = 2 * jp + hh
                lgh = lg[:, hh * TQ:(hh + 1) * TQ] - (2.0 ** -(h + 1)) * dist + bias
                m_old = m_ref[h]
                m_new = jnp.maximum(m_old, jnp.max(lgh, axis=1, keepdims=True))
                p = jnp.exp(lgh - m_new)
                alpha = jnp.exp(m_old - m_new)
                l_ref[h] = alpha * l_ref[h] + jnp.sum(p, axis=1, keepdims=True)
                m_ref[h] = m_new
                ps.append(p.astype(BF16))
                alphas.append(alpha)
            zv = _pair_blockdiag(vv[:, jp * LANES:(jp + 1) * LANES], lo_half)
            pv = _dot(jnp.concatenate(ps, axis=1), zv)
            acc_ref[jp] = acc_ref[jp] * jnp.where(lo_half, alphas[0], alphas[1]) + pv
        return 0

    lax.fori_loop(0, nkt, att_body, 0)
    for jp in range(A_HEADS // 2):
        denom = jnp.where(lo_half, l_ref[2 * jp], l_ref[2 * jp + 1])
        o_ref[0, :, jp * LANES:(jp + 1) * LANES] = (acc_ref[jp] / denom).astype(BF16)


def _attention(q, qi, wi, kx, k, v, topk):
    bsz, tp, _ = q.shape
    qtile = lambda w: pl.BlockSpec((1, TQ, w), lambda b, i: (b, i, 0))
    whole = lambda w: pl.BlockSpec((1, tp, w), lambda b, i: (b, 0, 0), pipeline_mode=pl.Buffered(1))
    return pl.pallas_call(
        functools.partial(_attn_kernel, topk=topk),
        grid=(bsz, tp // TQ),
        in_specs=[qtile(A_WIDTH), qtile(A_WIDTH), qtile(IDX_HEADS), whole(LANES), whole(A_WIDTH),
                  whole(A_WIDTH)],
        out_specs=qtile(A_WIDTH),
        out_shape=jax.ShapeDtypeStruct((bsz, tp, A_WIDTH), BF16),
        scratch_shapes=[pltpu.VMEM((TQ, tp), F32),
                        pltpu.VMEM((A_HEADS, TQ, 1), F32),
                        pltpu.VMEM((A_HEADS, TQ, 1), F32),
                        pltpu.VMEM((A_HEADS // 2, TQ, LANES), F32),
                        pltpu.VMEM((TQ, 8), F32),
                        pltpu.VMEM((TQ, TQ), F32),
                        pltpu.VMEM((TQ, 1), F32)],
        compiler_params=pltpu.CompilerParams(
            dimension_semantics=("arbitrary", "arbitrary"), vmem_limit_bytes=VMEM_LIMIT),
        name="attention",
    )(q, qi, wi, kx, k, v)


def _split_dot(x, w):
    hi = x.astype(BF16)
    lo = (x - hi.astype(F32)).astype(BF16)
    return _dot(hi, w) + _dot(lo, w)


def _softplus(x):
    return jnp.maximum(x, 0.0) + jnp.log1p(jnp.exp(-jnp.abs(x)))


def _sigmoid(x):
    return 1.0 / (1.0 + jnp.exp(-x))


def _rwkv_kernel(pb_ref, w0_ref, wda_ref, a0_ref, wg_ref, kk_ref, ka_ref, rk_ref, gng_ref, gnb_ref,
                 ones_ref, bdm_ref, msl_ref, mil_ref, tri_ref,
                 o_ref, state_ref, lw_s, r_s, kn_s, ba_s, km_s, v_s, y_s, *, tr):
    j = pl.program_id(1)

    @pl.when(j == 0)
    def _():
        state_ref[...] = jnp.zeros_like(state_ref)

    pb = pb_ref[0]
    r = pb[:, 0:B_WIDTH]
    k = pb[:, B_WIDTH:2 * B_WIDTH]
    v = pb[:, 2 * B_WIDTH:3 * B_WIDTH]
    da = pb[:, 3 * B_WIDTH:3 * B_WIDTH + LANES]
    is_decay = lax.broadcasted_iota(jnp.int32, (tr, LANES), 1) < DECAY_LORA
    pre = _dot(jnp.where(is_decay, jnp.tanh(da), da).astype(BF16), wda_ref[...])
    w_log = -_softplus(-(w0_ref[...] + pre[:, :B_WIDTH])) - 0.5
    a = _sigmoid(a0_ref[...] + pre[:, B_WIDTH:])
    kk = k * kk_ref[...]
    ones_bd = ones_ref[...]
    norm = jnp.sqrt(_split_dot(kk * kk, ones_bd))
    kn = kk / jnp.maximum(norm, 1e-12)
    km = k * (1.0 + (a - 1.0) * ka_ref[...])
    lw_s[...] = -jnp.exp(w_log)
    r_s[...] = r
    kn_s[...] = kn
    ba_s[...] = kn * a
    km_s[...] = km
    v_s[...] = v

    bdm = bdm_ref[...]
    bdm_b = bdm.astype(BF16)
    msl = msl_ref[...]
    mil = mil_ref[...]
    eye_w = mil - msl
    tri = tri_ref[...]

    def bd4(x):
        return jnp.concatenate([x, x, x, x], axis=0) * bdm_b

    def chunk_body(c, _):
        rows = pl.ds(pl.multiple_of(c * CHUNK, CHUNK), CHUNK)
        lw = lw_s[rows, :]
        cum = _split_dot_left(tri, lw)
        cum_end = cum[CHUNK - 1:CHUNK, :]
        e_pos = jnp.exp(cum)
        e_neg = jnp.exp(-cum)
        e_prev = jnp.exp(cum - lw)
        e_end = jnp.exp(cum_end - cum)
        g_end = jnp.exp(cum_end)
        rr = r_s[rows, :]
        kn_c = kn_s[rows, :]
        ba_c = ba_s[rows, :]
        km_c = km_s[rows, :]
        vv = v_s[rows, :]
        a_t = -kn_c * e_prev
        b_t = ba_c * e_neg
        k_t = km_c * e_neg
        r_t = rr * e_pos
        b_h = ba_c * e_end
        k_h = km_c * e_end
        for g in range(B_WIDTH // GROUP):
            sl = slice(g * GROUP, (g + 1) * GROUP)
            a_b = a_t[:, sl].astype(BF16)
            r_b = r_t[:, sl].astype(BF16)
            ar = jnp.concatenate([a_b, r_b], axis=0)
            gb = _dot_nt(ar, bd4(b_t[:, sl].astype(BF16)))
            gk = _dot_nt(ar, bd4(k_t[:, sl].astype(BF16)))
            l_ab = gb[:CHUNK] * msl
            l_ak = gk[:CHUNK] * msl
            g_rb = gb[CHUNK:] * mil
            g_rk = gk[CHUNK:] * mil
            x = eye_w + l_ab
            pw = l_ab
            for _ in range(5):
                pwb = pw.astype(BF16)
                pw = _dot(pwb, bd4(pwb))
                x = x + _dot(x.astype(BF16), bd4(pw.astype(BF16)))
            v_b = vv[:, sl].astype(BF16)
            v_bd = bd4(v_b)
            st = state_ref[g]
            st_b = st.astype(BF16)
            u_in = _dot_nt(a_b, st_b) + _dot(l_ak.astype(BF16), v_bd)
            u = _dot(x.astype(BF16), bd4(u_in.astype(BF16)))
            u_b = u.astype(BF16)
            y = _dot_nt(r_b, st_b) + _dot(g_rb.astype(BF16), bd4(u_b)) + _dot(g_rk.astype(BF16), v_bd)
            uv = jnp.concatenate([u_b, v_b], axis=0)
            bk = jnp.concatenate([b_h[:, sl], k_h[:, sl]], axis=0).astype(BF16)
            state_ref[g] = st * g_end[:, sl] + _dot_tn(uv, bk) * bdm
            y_s[rows, sl] = y
        return 0

    lax.fori_loop(0, tr // CHUNK, chunk_body, 0)

    y = y_s[...]
    inv = 1.0 / B_HEAD_DIM
    mu = _split_dot(y, ones_bd) * inv
    yc = y - mu
    var = _split_dot(yc * yc, ones_bd) * inv
    yn = yc * lax.rsqrt(var + GN_EPS) * gng_ref[...] + gnb_ref[...]
    bonus = _split_dot(r_s[...] * km_s[...] * rk_ref[...], ones_bd) * v_s[...]
    gd = pb_ref[0][:, 3 * B_WIDTH + LANES:]
    gate = _dot(_sigmoid(gd).astype(BF16), wg_ref[...])
    o_ref[0] = ((yn + bonus) * gate).astype(BF16)


def _split_dot_left(w, x):
    hi = x.astype(BF16)
    lo = (x - hi.astype(F32)).astype(BF16)
    return _dot(w, hi) + _dot(w, lo)


def _rwkv(pbs, w0, wda, a0, wg, k_k, k_a, r_k, gn_g, gn_b, ones_bd, bdm, msl, mil, tri):
    bsz, tp, _ = pbs.shape
    tr = _row_tile(tp)
    consts = [w0, wda, a0, wg, k_k, k_a, r_k, gn_g, gn_b, ones_bd, bdm, msl, mil, tri]
    tok = lambda w: pl.BlockSpec((1, tr, w), lambda b, j: (b, j, 0))
    return pl.pallas_call(
        functools.partial(_rwkv_kernel, tr=tr),
        grid=(bsz, tp // tr),
        in_specs=[tok(B_COLS)] + [_const_spec(c.shape) for c in consts],
        out_specs=tok(B_WIDTH),
        out_shape=jax.ShapeDtypeStruct((bsz, tp, B_WIDTH), BF16),
        scratch_shapes=[pltpu.VMEM((B_WIDTH // GROUP, GROUP, GROUP), F32)]
        + [pltpu.VMEM((tr, B_WIDTH), F32)] * 7,
        compiler_params=pltpu.CompilerParams(
            dimension_semantics=("arbitrary", "arbitrary"), vmem_limit_bytes=VMEM_LIMIT),
        name="rwkv",
    )(pbs, *consts)


def _gelu_tanh(x):
    return 0.5 * x * (1.0 + jnp.tanh(0.7978845608028654 * (x + 0.044715 * (x * x * x))))


def _ffn_kernel(x_ref, ya_ref, yb_ref, lng_ref, lnb_ref, wout_ref, l1g_ref, l1b_ref, wup_ref,
                cw_ref, cb_ref, wdn_ref, l2g_ref, l2b_ref, o_ref, hist_ref, *, tr):
    j = pl.program_id(1)

    @pl.when(j == 0)
    def _():
        hist_ref[...] = jnp.zeros_like(hist_ref)

    h0 = _layer_norm(x_ref[0], lng_ref[...], lnb_ref[...])
    mix = _dot(ya_ref[0], wout_ref[:A_WIDTH, :]) + _dot(yb_ref[0], wout_ref[A_WIDTH:, :])
    h1 = _layer_norm(ALPHA * h0 + mix, l1g_ref[...], l1b_ref[...])
    h1b = h1.astype(BF16)
    row = lax.broadcasted_iota(jnp.int32, (tr, 1), 0)
    valid = (j * tr + row) >= PAD
    acc = jnp.zeros((tr, D_MODEL), F32)
    for c in range(N_FF_CHUNKS):
        cs = slice(c * FF_CHUNK, (c + 1) * FF_CHUNK)
        gate = _dot(h1b, wup_ref[:, c * FF_CHUNK:(c + 1) * FF_CHUNK])
        val = _dot(h1b, wup_ref[:, D_FF + c * FF_CHUNK:D_FF + (c + 1) * FF_CHUNK])
        gate = jnp.where(valid, gate, 0.0)
        hist = hist_ref[c]
        g1 = jnp.where(row == 0, hist[7:8, :], pltpu.roll(gate, 1, 0))
        g2 = jnp.where(row == 0, hist[6:7, :], jnp.where(row == 1, hist[7:8, :], pltpu.roll(gate, 2, 0)))
        hist_ref[c] = gate[tr - 8:, :]
        conv = cw_ref[0:1, cs] * g2 + cw_ref[1:2, cs] * g1 + cw_ref[2:3, cs] * gate + cb_ref[:, cs]
        act = (_gelu_tanh(conv) * val).astype(BF16)
        acc = acc + _dot(act, wdn_ref[c * FF_CHUNK:(c + 1) * FF_CHUNK, :])
    o_ref[0] = _layer_norm(ALPHA * h1 + acc, l2g_ref[...], l2b_ref[...])


def _ffn(xcat, ya, yb, ln_g, ln_b, wout, l1g, l1b, wup, cw, cb, wdn, l2g, l2b):
    bsz, tp, _ = xcat.shape
    tr = _row_tile(tp)
    tok = lambda w: pl.BlockSpec((1, tr, w), lambda b, j: (b, j, 0))

    def res(c):
        nd = c.ndim
        return pl.BlockSpec(c.shape, lambda *_: (0,) * nd, pipeline_mode=pl.Buffered(1))

    return pl.pallas_call(
        functools.partial(_ffn_kernel, tr=tr),
        grid=(bsz, tp // tr),
        in_specs=[tok(D_MODEL), tok(A_WIDTH), tok(B_WIDTH), res(ln_g), res(ln_b), res(wout), res(l1g),
                  res(l1b), res(wup), res(cw), res(cb), res(wdn), res(l2g), res(l2b)],
        out_specs=tok(D_MODEL),
        out_shape=jax.ShapeDtypeStruct((bsz, tp, D_MODEL), F32),
        scratch_shapes=[pltpu.VMEM((N_FF_CHUNKS, 8, FF_CHUNK), F32)],
        compiler_params=pltpu.CompilerParams(
            dimension_semantics=("arbitrary", "arbitrary"), vmem_limit_bytes=VMEM_LIMIT),
        name="ffn",
    )(xcat, ya, yb, ln_g, ln_b, wout, l1g, l1b, wup, cw, cb, wdn, l2g, l2b)


def _row(v):
    return v.reshape(1, -1).astype(F32)


def kernel(x, meta, ln_in_g, ln_in_b, w_in, q_norm_g, w_uq, kv_norm_g, w_ukv, w_iq, idx_ln_g, idx_ln_b, mu_shift, w0, w_decay_up, a0, w_aaa_up, w_gate_up, k_k, k_a, r_k, gn_g, gn_b, w_out, ln1_g, ln1_b, w_ffn_up, conv_w, conv_b, w_ffn_down, ln2_g, ln2_b):
    bsz, seq, _ = x.shape
    assert seq % TQ == 0 and w_in.shape[0] == DEPTH
    t_real = N_META + seq
    topk = min(INDEX_TOPK, t_real // 4)

    xcat = jnp.concatenate([jnp.zeros((bsz, PAD, D_MODEL), x.dtype),
                            jnp.broadcast_to(meta[None].astype(x.dtype), (bsz, N_META, D_MODEL)), x], axis=1)

    wi = w_in[0]
    kidx_cols = wi[:, Q_RANK + KV_RANK:Q_RANK + KV_RANK + IDX_DIM]
    win = jnp.concatenate([
        wi[:, :Q_RANK + KV_RANK], kidx_cols, kidx_cols, wi[:, Q_RANK + KV_RANK + IDX_DIM:A_COLS],
        jnp.zeros((D_MODEL, A_PAD - 512 - IDX_HEADS), F32), wi[:, A_COLS:]], axis=1).astype(BF16)
    wukv = w_ukv[0].reshape(KV_RANK, A_HEADS, 2, A_HEAD_DIM)
    wuk = wukv[:, :, 0].reshape(KV_RANK, A_WIDTH).astype(BF16)
    wuv = wukv[:, :, 1].reshape(KV_RANK, A_WIDTH).astype(BF16)
    ig2 = _row(jnp.concatenate([idx_ln_g[0], idx_ln_g[0]]))
    ib2 = _row(jnp.concatenate([idx_ln_b[0], idx_ln_b[0]]))

    q, qi, widx, k, v, kx, pbs = _inproj(
        xcat, _row(ln_in_g), _row(ln_in_b), win, _row(q_norm_g[0]), w_uq[0].astype(BF16),
        w_iq[0].astype(BF16), _row(kv_norm_g[0]), wuk, wuv, ig2, ib2, _row(mu_shift[0]))

    ya = _attention(q, qi, widx, kx, k, v, topk)

    zl = jnp.zeros((DECAY_LORA, B_WIDTH), F32)
    wda = jnp.concatenate([jnp.concatenate([w_decay_up[0], zl], axis=1),
                           jnp.concatenate([zl, w_aaa_up[0]], axis=1)], axis=0).astype(BF16)
    ci = jnp.arange(B_WIDTH)
    ones_bd = (ci[:, None] // B_HEAD_DIM == ci[None, :] // B_HEAD_DIM).astype(BF16)
    gi = jnp.arange(GROUP)
    bdm = (gi[:, None] // B_HEAD_DIM == gi[None, :] // B_HEAD_DIM).astype(F32)
    ti = jnp.arange(CHUNK)
    msl = (ti[:, None] > (gi[None, :] % CHUNK)).astype(F32)
    mil = (ti[:, None] >= (gi[None, :] % CHUNK)).astype(F32)
    tri = (ti[:, None] >= ti[None, :]).astype(BF16)
    yb = _rwkv(pbs, _row(w0[0]), wda, _row(a0[0]), w_gate_up[0].astype(BF16), _row(k_k[0]), _row(k_a[0]),
               _row(r_k[0]), _row(gn_g[0]), _row(gn_b[0]), ones_bd, bdm, msl, mil, tri)

    out = _ffn(xcat, ya, yb, _row(ln_in_g), _row(ln_in_b), w_out[0].astype(BF16), _row(ln1_g[0]),
               _row(ln1_b[0]), w_ffn_up[0].astype(BF16), conv_w[0].astype(F32), _row(conv_b[0]),
               w_ffn_down[0].astype(BF16), _row(ln2_g[0]), _row(ln2_b[0]))
    return out[:, OFF:]
```

```python
import functools

import jax
import jax.numpy as jnp
from jax import lax
from jax.experimental import pallas as pl
from jax.experimental.pallas import tpu as pltpu

F32 = jnp.float32
BF16 = jnp.bfloat16

D_MODEL = 1024
N_META = 16
CHUNK = 64
A_HEADS = 8
A_HEAD_DIM = 64
A_WIDTH = A_HEADS * A_HEAD_DIM
Q_RANK = 256
KV_RANK = 128
IDX_HEADS = 8
IDX_DIM = 64
INDEX_TOPK = 256
B_HEADS = 8
B_HEAD_DIM = 64
B_WIDTH = B_HEADS * B_HEAD_DIM
DECAY_LORA = 64
AAA_LORA = 64
GATE_LORA = 128
GN_EPS = 64e-5
D_FF = 2816
LN_EPS = 1e-5
DEPTH = 1
ALPHA = (2 * DEPTH) ** 0.25
A_COLS = Q_RANK + KV_RANK + IDX_DIM + IDX_HEADS
B_COLS = 3 * B_WIDTH + DECAY_LORA + AAA_LORA + GATE_LORA
A_PAD = 640
IN_PAD = A_PAD + B_COLS

LANES = 128
TQ = 256
OFF = TQ
PAD = OFF - N_META
FF_CHUNK = 256
N_FF_CHUNKS = D_FF // FF_CHUNK
GROUP = 256
NEG = -1e30
LOG2E = 1.4426950408889634
VMEM_LIMIT = 56 * 1024 * 1024


def _row_tile(tp):
    for cand in (768, 512, 256):
        if tp % cand == 0:
            return cand
    raise ValueError(tp)


def _layer_norm(x, g, b):
    mu = jnp.mean(x, -1, keepdims=True)
    xc = x - mu
    var = jnp.mean(xc * xc, -1, keepdims=True)
    return xc * lax.rsqrt(var + LN_EPS) * g + b


def _dot(a, b):
    return jnp.dot(a, b, preferred_element_type=F32)


def _dot_nt(a, b):
    return lax.dot_general(a, b, (((1,), (1,)), ((), ())), preferred_element_type=F32)


def _dot_tn(a, b):
    return lax.dot_general(a, b, (((0,), (0,)), ((), ())), preferred_element_type=F32)


def _const_spec(shape):
    nd = len(shape)
    return pl.BlockSpec(shape, lambda *_: (0,) * nd)


def _inproj_kernel(x_ref, lng_ref, lnb_ref, win_ref, qg_ref, wuq_ref, wiq_ref, kvg_ref, wuk_ref,
                   wuvt_ref, ig_ref, ib_ref, mu_ref,
                   q_ref, qi_ref, wi_ref, k_ref, vt_ref, kx_ref, pb_ref, carry_ref, *, tr):
    j = pl.program_id(1)

    @pl.when(j == 0)
    def _():
        carry_ref[...] = jnp.zeros_like(carry_ref)

    h = _layer_norm(x_ref[0], lng_ref[...], lnb_ref[...])
    pos = j * tr + lax.broadcasted_iota(jnp.int32, (tr, 1), 0)
    valid = pos >= PAD
    h = jnp.where(valid, h, 0.0)
    p = _dot(h.astype(BF16), win_ref[...])

    cq = p[:, :Q_RANK]
    cq = cq * lax.rsqrt(jnp.mean(cq * cq, -1, keepdims=True) + 1e-6) * qg_ref[...]
    cqb = cq.astype(BF16)
    q_ref[0] = (_dot(cqb, wuq_ref[...]) * (A_HEAD_DIM ** -0.5 * LOG2E)).astype(BF16)
    qi_ref[0] = _dot(cqb, wiq_ref[...]).astype(BF16)
    ckv = p[:, Q_RANK:Q_RANK + KV_RANK]
    ckv = ckv * lax.rsqrt(jnp.mean(ckv * ckv, -1, keepdims=True) + 1e-6) * kvg_ref[...]
    ckvb = ckv.astype(BF16)
    k_ref[0] = _dot(ckvb, wuk_ref[...]).astype(BF16)
    vt_ref[0] = _dot_nt(wuvt_ref[...], ckvb).astype(BF16)
    kx_ref[0] = _layer_norm(p[:, 384:512], ig_ref[...], ib_ref[...]).astype(BF16)
    wi_ref[0] = p[:, 512:A_PAD].T[:IDX_HEADS, :] * (IDX_HEADS ** -0.5 * IDX_DIM ** -0.5)

    pb = p[:, A_PAD:]
    prev = carry_ref[7:8, :]
    row = lax.broadcasted_iota(jnp.int32, (tr, 1), 0)
    shifted = jnp.where(row == 0, prev, pltpu.roll(pb, 1, 0))
    carry_ref[...] = pb[tr - 8:, :]
    pbs = pb + (shifted - pb) * mu_ref[...]
    pb_ref[0] = jnp.where(valid, pbs, 0.0)


def _inproj(xcat, ln_g, ln_b, win, qg, wuq, wiq, kvg, wuk, wuvt, ig2, ib2, mu):
    bsz, tp, _ = xcat.shape
    tr = _row_tile(tp)
    tok = lambda w: pl.BlockSpec((1, tr, w), lambda b, j: (b, j, 0))
    chan = lambda c: pl.BlockSpec((1, c, tr), lambda b, j: (b, 0, j))
    consts = [ln_g, ln_b, win, qg, wuq, wiq, kvg, wuk, wuvt, ig2, ib2, mu]
    return pl.pallas_call(
        functools.partial(_inproj_kernel, tr=tr),
        grid=(bsz, tp // tr),
        in_specs=[tok(D_MODEL)] + [_const_spec(c.shape) for c in consts],
        out_specs=[tok(A_WIDTH), tok(A_WIDTH), chan(IDX_HEADS), tok(A_WIDTH), chan(A_WIDTH),
                   tok(LANES), tok(B_COLS)],
        out_shape=[jax.ShapeDtypeStruct((bsz, tp, A_WIDTH), BF16),
                   jax.ShapeDtypeStruct((bsz, tp, A_WIDTH), BF16),
                   jax.ShapeDtypeStruct((bsz, IDX_HEADS, tp), F32),
                   jax.ShapeDtypeStruct((bsz, tp, A_WIDTH), BF16),
                   jax.ShapeDtypeStruct((bsz, A_WIDTH, tp), BF16),
                   jax.ShapeDtypeStruct((bsz, tp, LANES), BF16),
                   jax.ShapeDtypeStruct((bsz, tp, B_COLS), F32)],
        scratch_shapes=[pltpu.VMEM((8, B_COLS), F32)],
        compiler_params=pltpu.CompilerParams(
            dimension_semantics=("arbitrary", "arbitrary"), vmem_limit_bytes=VMEM_LIMIT),
        name="inproj",
    )(xcat, *consts)


def _chunk_of(pos):
    return jnp.maximum((pos - OFF + CHUNK) >> 6, 0)


def _pair_blockdiag(slab, lo_half):
    zero = jnp.zeros_like(slab)
    return jnp.concatenate([jnp.where(lo_half, slab, zero), jnp.where(lo_half, zero, slab)], axis=0)


def _attn_kernel(q_ref, qi_ref, w_ref, kx_ref, k_ref, vt_ref, sd_ref, sdd_ref, vfirst_ref, vdiag_ref,
                 o_ref, s_ref, m_ref, l_ref, acc_ref, st_ref, bias_ref, tcar_ref, rmax_ref, rmin_ref,
                 *, topk):
    i = pl.program_id(1)
    nkt = i + 1
    lo_lane = lax.broadcasted_iota(jnp.int32, (TQ, LANES), 1) < A_HEAD_DIM
    lo_row = lax.broadcasted_iota(jnp.int32, (LANES, TQ), 0) < A_HEAD_DIM
    qp = i * TQ + lax.broadcasted_iota(jnp.int32, (1, TQ), 1)
    qc = _chunk_of(qp)
    kf = float(topk)

    qi = qi_ref[0]
    w = w_ref[0]

    def score_tile(kt, carry):
        rmax, rmin = carry
        base = pl.multiple_of(kt * TQ, TQ)
        z_idx = _pair_blockdiag(kx_ref[0, pl.ds(base, TQ), :], lo_lane)
        acc = jnp.zeros((TQ, TQ), F32)
        for jp in range(IDX_HEADS // 2):
            zz = _dot_nt(z_idx, qi[:, jp * LANES:(jp + 1) * LANES])
            r = jnp.maximum(zz, 0.0)
            acc = acc + r[:TQ] * w[2 * jp:2 * jp + 1] + r[TQ:] * w[2 * jp + 1:2 * jp + 2]
        s_ref[pl.ds(base, TQ), :] = acc
        rmax = jnp.maximum(rmax, jnp.max(acc, axis=0, keepdims=True))
        rmin = jnp.minimum(rmin, jnp.min(acc, axis=0, keepdims=True))
        return rmax, rmin

    ext = score_tile(0, (jnp.full((1, TQ), -jnp.inf, F32), jnp.full((1, TQ), jnp.inf, F32)))
    ext = lax.fori_loop(0, i // 2, lambda g, c: score_tile(2 * g + 2, score_tile(2 * g + 1, c)), ext)
    rmax_ref[...] = ext[0]
    rmin_ref[...] = ext[1]

    @pl.when((i & 1) == 1)
    def _():
        last = score_tile(i, (rmax_ref[...], rmin_ref[...]))
        rmax_ref[...] = last[0]
        rmin_ref[...] = last[1]

    rmax = rmax_ref[...]
    rmin = rmin_ref[...]
    s_ref[0:TQ, :] = s_ref[0:TQ, :] + vfirst_ref[...]

    @pl.when(i > 0)
    def _():
        dbase = pl.multiple_of(i * TQ, TQ)
        s_ref[pl.ds(dbase, TQ), :] = s_ref[pl.ds(dbase, TQ), :] + vdiag_ref[...]

    s_ref[pl.ds(pl.multiple_of(nkt * TQ, TQ), TQ), :] = jnp.full((TQ, TQ), -jnp.inf, F32)

    def count_above(thr, strict=True):
        def cnt_body(g, part):
            for half in range(2):
                base = pl.multiple_of(g * (2 * TQ) + half * TQ, TQ)
                for r0 in range(0, TQ, 64):
                    s = s_ref[pl.ds(base + r0, 64), :]
                    hit = jnp.where(s > thr if strict else s >= thr, 1.0, 0.0)
                    part = part + jnp.sum(hit.reshape(8, 8, TQ), axis=0)
            return part

        part = lax.fori_loop(0, (nkt + 1) // 2, cnt_body, jnp.zeros((8, TQ), F32))
        return jnp.sum(part, axis=0, keepdims=True)

    nvis = (N_META + CHUNK * qc).astype(F32)
    zero = jnp.zeros((1, TQ), F32)
    c_gt0 = count_above(zero)
    c_ge0 = count_above(zero, strict=False)
    all_visible = nvis <= kf
    at_zero = jnp.logical_and(c_gt0 <= kf, c_ge0 >= kf)
    positive = c_gt0 > kf
    below_min = rmin - jnp.maximum(jnp.abs(rmin) * (2.0 ** -20), 1e-30)
    st_ref[0:1, :] = jnp.where(positive, zero, below_min)
    st_ref[1:2, :] = jnp.where(positive, rmax, zero)
    st_ref[3:4, :] = jnp.where(positive, zero, c_gt0)
    st_ref[2:3, :] = jnp.where(positive, c_gt0, nvis) - kf
    st_ref[7:8, :] = kf - jnp.where(positive, zero, c_gt0)
    st_ref[8:9, :] = zero
    st_ref[4:5, :] = jnp.where(all_visible, -jnp.inf, zero)
    st_ref[5:6, :] = jnp.where(jnp.logical_and(at_zero, jnp.logical_not(all_visible)), kf - c_gt0, zero)
    st_ref[6:7, :] = jnp.where(jnp.logical_or(all_visible, at_zero), 1.0, 0.0)

    def search_cond(c):
        it, active = c
        return jnp.logical_and(active > 0.0, it < 400)

    def search_body(c):
        it, _ = c
        lo = st_ref[0:1, :]
        hi = st_ref[1:2, :]
        chi = st_ref[3:4, :]
        f_lo = st_ref[2:3, :]
        g_hi = st_ref[7:8, :]
        side = st_ref[8:9, :]
        thr = st_ref[4:5, :]
        need = st_ref[5:6, :]
        done = st_ref[6:7, :]
        sign = jnp.where(hi > 0.0, 1.0, -1.0)
        small = jnp.minimum(jnp.abs(lo), jnp.abs(hi))
        large = jnp.maximum(jnp.abs(lo), jnp.abs(hi))
        geo = sign * jnp.where(small == 0.0, large * (2.0 ** -12), jnp.sqrt(small * large))
        halve = jnp.where(large > 4.0 * small, geo, lo + (hi - lo) * 0.5)
        lin = lo + (hi - lo) * jnp.clip(f_lo / (f_lo + g_hi), 0.002, 0.998)
        use_lin = jnp.logical_and(jnp.logical_and(lin > lo, lin < hi), (it & 3) != 3)
        mid = jnp.where(use_lin, lin, halve)
        inside = jnp.logical_and(mid > lo, mid < hi)
        cnt = count_above(mid)
        nd = done < 0.5
        nd_in = jnp.logical_and(nd, inside)
        exact = jnp.logical_and(nd_in, cnt == kf)
        coll = jnp.logical_and(nd, jnp.logical_not(inside))
        go_lo = jnp.logical_and(nd_in, cnt > kf)
        go_hi = jnp.logical_and(nd_in, cnt < kf)
        st_ref[4:5, :] = jnp.where(exact, mid, jnp.where(coll, hi, thr))
        st_ref[5:6, :] = jnp.where(coll, kf - chi, need)
        done = jnp.where(jnp.logical_or(exact, coll), 1.0, done)
        st_ref[6:7, :] = done
        st_ref[0:1, :] = jnp.where(go_lo, mid, lo)
        st_ref[1:2, :] = jnp.where(go_hi, mid, hi)
        st_ref[3:4, :] = jnp.where(go_hi, cnt, chi)
        st_ref[2:3, :] = jnp.where(go_lo, cnt - kf,
                                   jnp.where(jnp.logical_and(go_hi, side < 0.0), f_lo * 0.5, f_lo))
        st_ref[7:8, :] = jnp.where(go_hi, kf - cnt,
                                   jnp.where(jnp.logical_and(go_lo, side > 0.0), g_hi * 0.5, g_hi))
        st_ref[8:9, :] = jnp.where(go_lo, 1.0, jnp.where(go_hi, -1.0, side))
        return it + 1, jnp.max(1.0 - done)

    lax.while_loop(search_cond, search_body, (jnp.int32(0), jnp.max(1.0 - st_ref[6:7, :])))
    unfinished = st_ref[6:7, :] < 0.5
    thr = jnp.where(unfinished, st_ref[1:2, :], st_ref[4:5, :])
    need = jnp.where(unfinished, kf - st_ref[3:4, :], st_ref[5:6, :])
    tie_any = jnp.max(need) > 0.0

    m_ref[...] = jnp.full(m_ref.shape, NEG, F32)
    l_ref[...] = jnp.zeros_like(l_ref)
    acc_ref[...] = jnp.zeros_like(acc_ref)
    tcar_ref[...] = jnp.zeros_like(tcar_ref)
    q = q_ref[0]

    def select_bias(kt, slot):
        base = pl.multiple_of(kt * TQ, TQ)
        s = s_ref[pl.ds(base, TQ), :]
        bias_ref[slot] = jnp.where(s > thr, 0.0, NEG)

        @pl.when(tie_any)
        def _():
            tie = jnp.logical_and(s == thr, need > 0.0)
            tie_b = jnp.where(tie, 1.0, 0.0).astype(BF16)
            lower = (lax.broadcasted_iota(jnp.int32, (TQ, TQ), 1)
                     <= lax.broadcasted_iota(jnp.int32, (TQ, TQ), 0))
            prefix = _dot(jnp.where(lower, 1.0, 0.0).astype(BF16), tie_b)
            rank = tcar_ref[...] + prefix
            take = jnp.logical_and(tie, rank <= need)
            bias_ref[slot] = jnp.where(jnp.logical_or(s > thr, take), 0.0, NEG)
            tcar_ref[...] = tcar_ref[...] + prefix[TQ - 1:TQ, :]

    def att_tiles(kts, diag):
        for slot, kt in enumerate(kts):
            select_bias(kt, slot)
        bases = [pl.multiple_of(kt * TQ, TQ) for kt in kts]
        kks = [k_ref[0, pl.ds(base, TQ), :] for base in bases]
        gaps = [((i - kt) * TQ).astype(F32) for kt in kts]
        for jp in range(A_HEADS // 2):
            cols = slice(jp * LANES, (jp + 1) * LANES)
            zk = jnp.concatenate([_pair_blockdiag(kk[:, cols], lo_lane) for kk in kks], axis=0)
            lg = _dot_nt(zk, q[:, cols])
            ps, alphas = [[None, None] for _ in kts], []
            for hh in range(2):
                h = 2 * jp + hh
                lghs, shifts = [], []
                for t in range(len(kts)):
                    r0 = (2 * t + hh) * TQ
                    lghs.append(lg[r0:r0 + TQ] - (sdd_ref[h] if diag else sd_ref[h]) + bias_ref[t])
                    shifts.append(0.0 if diag else (LOG2E * 2.0 ** -(h + 1)) * gaps[t])
                m_old = m_ref[h:h + 1, :]
                m_new = m_old
                for lgh, shift in zip(lghs, shifts):
                    m_new = jnp.maximum(m_new, jnp.max(lgh, axis=0, keepdims=True) - shift)
                alpha = jnp.exp2(m_old - m_new)
                l_new = alpha * l_ref[h:h + 1, :]
                for t, (lgh, shift) in enumerate(zip(lghs, shifts)):
                    p = jnp.exp2(lgh - (m_new + shift))
                    l_new = l_new + jnp.sum(p, axis=0, keepdims=True)
                    ps[t][hh] = p.astype(BF16)
                l_ref[h:h + 1, :] = l_new
                m_ref[h:h + 1, :] = m_new
                alphas.append(alpha)
            zvts = []
            for base in bases:
                vt = vt_ref[0, cols, pl.ds(base, TQ)]
                zero = jnp.zeros_like(vt)
                zvts += [jnp.where(lo_row, vt, zero), jnp.where(lo_row, zero, vt)]
            pv = _dot(jnp.concatenate(zvts, axis=1),
                      jnp.concatenate([p for pt in ps for p in pt], axis=0))
            acc_ref[jp] = acc_ref[jp] * jnp.where(lo_row, alphas[0], alphas[1]) + pv

    def att_body(g, _):
        att_tiles([2 * g, 2 * g + 1], False)
        return 0

    lax.fori_loop(0, i // 2, att_body, 0)

    @pl.when((i & 1) == 1)
    def _():
        att_tiles([i - 1], False)

    att_tiles([i], True)
    for jp in range(A_HEADS // 2):
        denom = jnp.where(lo_row, l_ref[2 * jp:2 * jp + 1, :], l_ref[2 * jp + 1:2 * jp + 2, :])
        o_ref[0, :, jp * LANES:(jp + 1) * LANES] = (acc_ref[jp] / denom).T.astype(BF16)


def _attention(q, qi, wi, kx, k, vt, topk):
    bsz, tp, _ = q.shape
    qtile = lambda w: pl.BlockSpec((1, TQ, w), lambda b, i: (b, i, 0))
    whole = lambda r, c: pl.BlockSpec((1, r, c), lambda b, i: (b, 0, 0), pipeline_mode=pl.Buffered(1))
    kq = jnp.arange(TQ)
    rel = (kq[None, :] - kq[:, None]).astype(F32)
    slopes = (LOG2E * 2.0 ** -(jnp.arange(A_HEADS) + 1.0)).astype(F32)
    sd = slopes[:, None, None] * rel[None]
    sdd = slopes[:, None, None] * jnp.abs(rel)[None]
    vfirst = jnp.where(kq[:, None] >= PAD, 0.0, -jnp.inf) * jnp.ones((1, TQ), F32)
    vdiag = jnp.where((kq[:, None] // CHUNK) <= (kq[None, :] // CHUNK), 0.0, -jnp.inf).astype(F32)
    consts = [sd, sdd, vfirst.astype(F32), vdiag]
    return pl.pallas_call(
        functools.partial(_attn_kernel, topk=topk),
        grid=(bsz, tp // TQ),
        in_specs=[qtile(A_WIDTH), qtile(A_WIDTH),
                  pl.BlockSpec((1, IDX_HEADS, TQ), lambda b, i: (b, 0, i)),
                  whole(tp, LANES), whole(tp, A_WIDTH), whole(A_WIDTH, tp)]
        + [pl.BlockSpec(c.shape, lambda b, i, nd=c.ndim: (0,) * nd, pipeline_mode=pl.Buffered(1))
           for c in consts],
        out_specs=qtile(A_WIDTH),
        out_shape=jax.ShapeDtypeStruct((bsz, tp, A_WIDTH), BF16),
        scratch_shapes=[pltpu.VMEM((tp + TQ, TQ), F32),
                        pltpu.VMEM((A_HEADS, TQ), F32),
                        pltpu.VMEM((A_HEADS, TQ), F32),
                        pltpu.VMEM((A_HEADS // 2, LANES, TQ), F32),
                        pltpu.VMEM((16, TQ), F32),
                        pltpu.VMEM((2, TQ, TQ), F32),
                        pltpu.VMEM((1, TQ), F32),
                        pltpu.VMEM((1, TQ), F32),
                        pltpu.VMEM((1, TQ), F32)],
        compiler_params=pltpu.CompilerParams(
            dimension_semantics=("arbitrary", "arbitrary"), vmem_limit_bytes=VMEM_LIMIT),
        name="attention",
    )(q, qi, wi, kx, k, vt, *consts)


def _split_dot(x, w):
    hi = x.astype(BF16)
    lo = (x - hi.astype(F32)).astype(BF16)
    return _dot(hi, w) + _dot(lo, w)


def _softplus(x):
    return jnp.maximum(x, 0.0) + jnp.log1p(jnp.exp(-jnp.abs(x)))


def _sigmoid(x):
    return 1.0 / (1.0 + jnp.exp(-x))


def _rwkv_kernel(pb_ref, w0_ref, wda_ref, a0_ref, wg_ref, kk_ref, ka_ref, rk_ref, gng_ref, gnb_ref,
                 ones_ref, bdm_ref, msl_ref, mil_ref, tri_ref,
                 o_ref, state_ref, lw_s, r_s, kn_s, ba_s, km_s, v_s, y_s, *, tr):
    j = pl.program_id(1)

    @pl.when(j == 0)
    def _():
        state_ref[...] = jnp.zeros_like(state_ref)

    pb = pb_ref[0]
    r = pb[:, 0:B_WIDTH]
    k = pb[:, B_WIDTH:2 * B_WIDTH]
    v = pb[:, 2 * B_WIDTH:3 * B_WIDTH]
    da = pb[:, 3 * B_WIDTH:3 * B_WIDTH + LANES]
    is_decay = lax.broadcasted_iota(jnp.int32, (tr, LANES), 1) < DECAY_LORA
    pre = _dot(jnp.where(is_decay, jnp.tanh(da), da).astype(BF16), wda_ref[...])
    w_log = -_softplus(-(w0_ref[...] + pre[:, :B_WIDTH])) - 0.5
    a = _sigmoid(a0_ref[...] + pre[:, B_WIDTH:])
    kk = k * kk_ref[...]
    ones_bd = ones_ref[...]
    norm = jnp.sqrt(_split_dot(kk * kk, ones_bd))
    kn = kk / jnp.maximum(norm, 1e-12)
    km = k * (1.0 + (a - 1.0) * ka_ref[...])
    lw_s[...] = -jnp.exp(w_log)
    r_s[...] = r
    kn_s[...] = kn
    ba_s[...] = kn * a
    km_s[...] = km
    v_s[...] = v

    bdm = bdm_ref[...]
    bdm_b = bdm.astype(BF16)
    msl = msl_ref[...]
    mil = mil_ref[...]
    eye_w = mil - msl
    tri = tri_ref[...]

    def bd4(x):
        return jnp.concatenate([x, x, x, x], axis=0) * bdm_b

    def chunk_body(c, _):
        rows = pl.ds(pl.multiple_of(c * CHUNK, CHUNK), CHUNK)
        lw = lw_s[rows, :]
        cum = _split_dot_left(tri, lw)
        cum_end = cum[CHUNK - 1:CHUNK, :]
        e_pos = jnp.exp(cum)
        e_neg = jnp.exp(-cum)
        e_prev = jnp.exp(cum - lw)
        e_end = jnp.exp(cum_end - cum)
        g_end = jnp.exp(cum_end)
        rr = r_s[rows, :]
        kn_c = kn_s[rows, :]
        ba_c = ba_s[rows, :]
        km_c = km_s[rows, :]
        vv = v_s[rows, :]
        a_t = -kn_c * e_prev
        b_t = ba_c * e_neg
        k_t = km_c * e_neg
        r_t = rr * e_pos
        b_h = ba_c * e_end
        k_h = km_c * e_end
        for g in range(B_WIDTH // GROUP):
            sl = slice(g * GROUP, (g + 1) * GROUP)
            a_b = a_t[:, sl].astype(BF16)
            r_b = r_t[:, sl].astype(BF16)
            ar = jnp.concatenate([a_b, r_b], axis=0)
            gb = _dot_nt(ar, bd4(b_t[:, sl].astype(BF16)))
            gk = _dot_nt(ar, bd4(k_t[:, sl].astype(BF16)))
            l_ab = gb[:CHUNK] * msl
            l_ak = gk[:CHUNK] * msl
            g_rb = gb[CHUNK:] * mil
            g_rk = gk[CHUNK:] * mil
            x = eye_w + l_ab
            pw = l_ab
            for _ in range(5):
                pwb = pw.astype(BF16)
                pw = _dot(pwb, bd4(pwb))
                x = x + _dot(x.astype(BF16), bd4(pw.astype(BF16)))
            v_b = vv[:, sl].astype(BF16)
            v_bd = bd4(v_b)
            st = state_ref[g]
            st_b = st.astype(BF16)
            u_in = _dot_nt(a_b, st_b) + _dot(l_ak.astype(BF16), v_bd)
            u = _dot(x.astype(BF16), bd4(u_in.astype(BF16)))
            u_b = u.astype(BF16)
            y = _dot_nt(r_b, st_b) + _dot(g_rb.astype(BF16), bd4(u_b)) + _dot(g_rk.astype(BF16), v_bd)
            uv = jnp.concatenate([u_b, v_b], axis=0)
            bk = jnp.concatenate([b_h[:, sl], k_h[:, sl]], axis=0).astype(BF16)
            state_ref[g] = st * g_end[:, sl] + _dot_tn(uv, bk) * bdm
            y_s[rows, sl] = y
        return 0

    lax.fori_loop(0, tr // CHUNK, chunk_body, 0)

    y = y_s[...]
    inv = 1.0 / B_HEAD_DIM
    mu = _split_dot(y, ones_bd) * inv
    yc = y - mu
    var = _split_dot(yc * yc, ones_bd) * inv
    yn = yc * lax.rsqrt(var + GN_EPS) * gng_ref[...] + gnb_ref[...]
    bonus = _split_dot(r_s[...] * km_s[...] * rk_ref[...], ones_bd) * v_s[...]
    gd = pb_ref[0][:, 3 * B_WIDTH + LANES:]
    gate = _dot(_sigmoid(gd).astype(BF16), wg_ref[...])
    o_ref[0] = ((yn + bonus) * gate).astype(BF16)


def _split_dot_left(w, x):
    hi = x.astype(BF16)
    lo = (x - hi.astype(F32)).astype(BF16)
    return _dot(w, hi) + _dot(w, lo)


def _rwkv(pbs, w0, wda, a0, wg, k_k, k_a, r_k, gn_g, gn_b, ones_bd, bdm, msl, mil, tri):
    bsz, tp, _ = pbs.shape
    tr = _row_tile(tp)
    consts = [w0, wda, a0, wg, k_k, k_a, r_k, gn_g, gn_b, ones_bd, bdm, msl, mil, tri]
    tok = lambda w: pl.BlockSpec((1, tr, w), lambda b, j: (b, j, 0))
    return pl.pallas_call(
        functools.partial(_rwkv_kernel, tr=tr),
        grid=(bsz, tp // tr),
        in_specs=[tok(B_COLS)] + [_const_spec(c.shape) for c in consts],
        out_specs=tok(B_WIDTH),
        out_shape=jax.ShapeDtypeStruct((bsz, tp, B_WIDTH), BF16),
        scratch_shapes=[pltpu.VMEM((B_WIDTH // GROUP, GROUP, GROUP), F32)]
        + [pltpu.VMEM((tr, B_WIDTH), F32)] * 7,
        compiler_params=pltpu.CompilerParams(
            dimension_semantics=("arbitrary", "arbitrary"), vmem_limit_bytes=VMEM_LIMIT),
        name="rwkv",
    )(pbs, *consts)


def _gelu_tanh(x):
    return 0.5 * x * (1.0 + jnp.tanh(0.7978845608028654 * (x + 0.044715 * (x * x * x))))


def _ffn_kernel(x_ref, ya_ref, yb_ref, lng_ref, lnb_ref, wout_ref, l1g_ref, l1b_ref, wup_ref,
                cw_ref, cb_ref, wdn_ref, l2g_ref, l2b_ref, o_ref, hist_ref, *, tr):
    j = pl.program_id(1)

    @pl.when(j == 0)
    def _():
        hist_ref[...] = jnp.zeros_like(hist_ref)

    h0 = _layer_norm(x_ref[0], lng_ref[...], lnb_ref[...])
    mix = _dot(ya_ref[0], wout_ref[:A_WIDTH, :]) + _dot(yb_ref[0], wout_ref[A_WIDTH:, :])
    h1 = _layer_norm(ALPHA * h0 + mix, l1g_ref[...], l1b_ref[...])
    h1b = h1.astype(BF16)
    row = lax.broadcasted_iota(jnp.int32, (tr, 1), 0)
    valid = (j * tr + row) >= PAD
    acc = jnp.zeros((tr, D_MODEL), F32)
    for c in range(N_FF_CHUNKS):
        cs = slice(c * FF_CHUNK, (c + 1) * FF_CHUNK)
        gate = _dot(h1b, wup_ref[:, c * FF_CHUNK:(c + 1) * FF_CHUNK])
        val = _dot(h1b, wup_ref[:, D_FF + c * FF_CHUNK:D_FF + (c + 1) * FF_CHUNK])
        gate = jnp.where(valid, gate, 0.0)
        hist = hist_ref[c]
        g1 = jnp.where(row == 0, hist[7:8, :], pltpu.roll(gate, 1, 0))
        g2 = jnp.where(row == 0, hist[6:7, :], jnp.where(row == 1, hist[7:8, :], pltpu.roll(gate, 2, 0)))
        hist_ref[c] = gate[tr - 8:, :]
        conv = cw_ref[0:1, cs] * g2 + cw_ref[1:2, cs] * g1 + cw_ref[2:3, cs] * gate + cb_ref[:, cs]
        act = (_gelu_tanh(conv) * val).astype(BF16)
        acc = acc + _dot(act, wdn_ref[c * FF_CHUNK:(c + 1) * FF_CHUNK, :])
    o_ref[0] = _layer_norm(ALPHA * h1 + acc, l2g_ref[...], l2b_ref[...])


def _ffn(xcat, ya, yb, ln_g, ln_b, wout, l1g, l1b, wup, cw, cb, wdn, l2g, l2b):
    bsz, tp, _ = xcat.shape
    tr = _row_tile(tp)
    tok = lambda w: pl.BlockSpec((1, tr, w), lambda b, j: (b, j, 0))

    def res(c):
        nd = c.ndim
        return pl.BlockSpec(c.shape, lambda *_: (0,) * nd, pipeline_mode=pl.Buffered(1))

    return pl.pallas_call(
        functools.partial(_ffn_kernel, tr=tr),
        grid=(bsz, tp // tr),
        in_specs=[tok(D_MODEL), tok(A_WIDTH), tok(B_WIDTH), res(ln_g), res(ln_b), res(wout), res(l1g),
                  res(l1b), res(wup), res(cw), res(cb), res(wdn), res(l2g), res(l2b)],
        out_specs=tok(D_MODEL),
        out_shape=jax.ShapeDtypeStruct((bsz, tp, D_MODEL), F32),
        scratch_shapes=[pltpu.VMEM((N_FF_CHUNKS, 8, FF_CHUNK), F32)],
        compiler_params=pltpu.CompilerParams(
            dimension_semantics=("arbitrary", "arbitrary"), vmem_limit_bytes=VMEM_LIMIT),
        name="ffn",
    )(xcat, ya, yb, ln_g, ln_b, wout, l1g, l1b, wup, cw, cb, wdn, l2g, l2b)


def _row(v):
    return v.reshape(1, -1).astype(F32)


def kernel(x, meta, ln_in_g, ln_in_b, w_in, q_norm_g, w_uq, kv_norm_g, w_ukv, w_iq, idx_ln_g, idx_ln_b, mu_shift, w0, w_decay_up, a0, w_aaa_up, w_gate_up, k_k, k_a, r_k, gn_g, gn_b, w_out, ln1_g, ln1_b, w_ffn_up, conv_w, conv_b, w_ffn_down, ln2_g, ln2_b):
    bsz, seq, _ = x.shape
    assert seq % TQ == 0 and w_in.shape[0] == DEPTH
    t_real = N_META + seq
    topk = min(INDEX_TOPK, t_real // 4)

    xcat = jnp.concatenate([jnp.zeros((bsz, PAD, D_MODEL), x.dtype),
                            jnp.broadcast_to(meta[None].astype(x.dtype), (bsz, N_META, D_MODEL)), x], axis=1)

    wi = w_in[0]
    kidx_cols = wi[:, Q_RANK + KV_RANK:Q_RANK + KV_RANK + IDX_DIM]
    win = jnp.concatenate([
        wi[:, :Q_RANK + KV_RANK], kidx_cols, kidx_cols, wi[:, Q_RANK + KV_RANK + IDX_DIM:A_COLS],
        jnp.zeros((D_MODEL, A_PAD - 512 - IDX_HEADS), F32), wi[:, A_COLS:]], axis=1).astype(BF16)
    wukv = w_ukv[0].reshape(KV_RANK, A_HEADS, 2, A_HEAD_DIM)
    wuk = wukv[:, :, 0].reshape(KV_RANK, A_WIDTH).astype(BF16)
    wuvt = wukv[:, :, 1].reshape(KV_RANK, A_WIDTH).T.astype(BF16)
    ig2 = _row(jnp.concatenate([idx_ln_g[0], idx_ln_g[0]]))
    ib2 = _row(jnp.concatenate([idx_ln_b[0], idx_ln_b[0]]))

    q, qi, widx, k, vt, kx, pbs = _inproj(
        xcat, _row(ln_in_g), _row(ln_in_b), win, _row(q_norm_g[0]), w_uq[0].astype(BF16),
        w_iq[0].astype(BF16), _row(kv_norm_g[0]), wuk, wuvt, ig2, ib2, _row(mu_shift[0]))

    ya = _attention(q, qi, widx, kx, k, vt, topk)

    zl = jnp.zeros((DECAY_LORA, B_WIDTH), F32)
    wda = jnp.concatenate([jnp.concatenate([w_decay_up[0], zl], axis=1),
                           jnp.concatenate([zl, w_aaa_up[0]], axis=1)], axis=0).astype(BF16)
    ci = jnp.arange(B_WIDTH)
    ones_bd = (ci[:, None] // B_HEAD_DIM == ci[None, :] // B_HEAD_DIM).astype(BF16)
    gi = jnp.arange(GROUP)
    bdm = (gi[:, None] // B_HEAD_DIM == gi[None, :] // B_HEAD_DIM).astype(F32)
    ti = jnp.arange(CHUNK)
    msl = (ti[:, None] > (gi[None, :] % CHUNK)).astype(F32)
    mil = (ti[:, None] >= (gi[None, :] % CHUNK)).astype(F32)
    tri = (ti[:, None] >= ti[None, :]).astype(BF16)
    yb = _rwkv(pbs, _row(w0[0]), wda, _row(a0[0]), w_gate_up[0].astype(BF16), _row(k_k[0]), _row(k_a[0]),
               _row(r_k[0]), _row(gn_g[0]), _row(gn_b[0]), ones_bd, bdm, msl, mil, tri)

    out = _ffn(xcat, ya, yb, _row(ln_in_g), _row(ln_in_b), w_out[0].astype(BF16), _row(ln1_g[0]),
               _row(ln1_b[0]), w_ffn_up[0].astype(BF16), conv_w[0].astype(F32), _row(conv_b[0]),
               w_ffn_down[0].astype(BF16), _row(ln2_g[0]), _row(ln2_b[0]))
    return out[:, OFF:]
```

```python
import functools

import jax
import jax.numpy as jnp
from jax import lax
from jax.experimental import pallas as pl
from jax.experimental.pallas import tpu as pltpu

F32 = jnp.float32
BF16 = jnp.bfloat16

D_MODEL = 1024
N_META = 16
CHUNK = 64
A_HEADS = 8
A_HEAD_DIM = 64
A_WIDTH = A_HEADS * A_HEAD_DIM
Q_RANK = 256
KV_RANK = 128
IDX_HEADS = 8
IDX_DIM = 64
INDEX_TOPK = 256
B_HEADS = 8
B_HEAD_DIM = 64
B_WIDTH = B_HEADS * B_HEAD_DIM
DECAY_LORA = 64
AAA_LORA = 64
GATE_LORA = 128
GN_EPS = 64e-5
D_FF = 2816
LN_EPS = 1e-5
DEPTH = 1
ALPHA = (2 * DEPTH) ** 0.25
A_COLS = Q_RANK + KV_RANK + IDX_DIM + IDX_HEADS
B_COLS = 3 * B_WIDTH + DECAY_LORA + AAA_LORA + GATE_LORA
A_PAD = 640
IN_PAD = A_PAD + B_COLS

LANES = 128
TQ = 256
OFF = TQ
PAD = OFF - N_META
FF_CHUNK = 256
N_FF_CHUNKS = D_FF // FF_CHUNK
GROUP = 256
NEG = -1e30
LOG2E = 1.4426950408889634
VMEM_LIMIT = 56 * 1024 * 1024


def _row_tile(tp):
    for cand in (768, 512, 256):
        if tp % cand == 0:
            return cand
    raise ValueError(tp)


def _layer_norm(x, g, b):
    mu = jnp.mean(x, -1, keepdims=True)
    xc = x - mu
    var = jnp.mean(xc * xc, -1, keepdims=True)
    return xc * lax.rsqrt(var + LN_EPS) * g + b


def _dot(a, b):
    return jnp.dot(a, b, preferred_element_type=F32)


def _dot_nt(a, b):
    return lax.dot_general(a, b, (((1,), (1,)), ((), ())), preferred_element_type=F32)


def _dot_tn(a, b):
    return lax.dot_general(a, b, (((0,), (0,)), ((), ())), preferred_element_type=F32)


def _const_spec(shape):
    nd = len(shape)
    return pl.BlockSpec(shape, lambda *_: (0,) * nd)


def _inproj_kernel(x_ref, lng_ref, lnb_ref, win_ref, qg_ref, wuq_ref, wiq_ref, kvg_ref, wuk_ref,
                   wuvt_ref, ig_ref, ib_ref, mu_ref,
                   q_ref, qi_ref, wi_ref, k_ref, vt_ref, kx_ref, pb_ref, carry_ref, *, tr):
    j = pl.program_id(1)

    @pl.when(j == 0)
    def _():
        carry_ref[...] = jnp.zeros_like(carry_ref)

    h = _layer_norm(x_ref[0], lng_ref[...], lnb_ref[...])
    pos = j * tr + lax.broadcasted_iota(jnp.int32, (tr, 1), 0)
    valid = pos >= PAD
    h = jnp.where(valid, h, 0.0)
    p = _dot(h.astype(BF16), win_ref[...])

    cq = p[:, :Q_RANK]
    cq = cq * lax.rsqrt(jnp.mean(cq * cq, -1, keepdims=True) + 1e-6) * qg_ref[...]
    cqb = cq.astype(BF16)
    q_ref[0] = (_dot(cqb, wuq_ref[...]) * (A_HEAD_DIM ** -0.5 * LOG2E)).astype(BF16)
    qi_ref[0] = _dot(cqb, wiq_ref[...]).astype(BF16)
    ckv = p[:, Q_RANK:Q_RANK + KV_RANK]
    ckv = ckv * lax.rsqrt(jnp.mean(ckv * ckv, -1, keepdims=True) + 1e-6) * kvg_ref[...]
    ckvb = ckv.astype(BF16)
    k_ref[0] = _dot(ckvb, wuk_ref[...]).astype(BF16)
    vt_ref[0] = _dot_nt(wuvt_ref[...], ckvb).astype(BF16)
    kx_ref[0] = _layer_norm(p[:, 384:512], ig_ref[...], ib_ref[...]).astype(BF16)
    wi_ref[0] = p[:, 512:A_PAD].T[:IDX_HEADS, :] * (IDX_HEADS ** -0.5 * IDX_DIM ** -0.5)

    pb = p[:, A_PAD:]
    prev = carry_ref[7:8, :]
    row = lax.broadcasted_iota(jnp.int32, (tr, 1), 0)
    shifted = jnp.where(row == 0, prev, pltpu.roll(pb, 1, 0))
    carry_ref[...] = pb[tr - 8:, :]
    pbs = pb + (shifted - pb) * mu_ref[...]
    pb_ref[0] = jnp.where(valid, pbs, 0.0)


def _inproj(xcat, ln_g, ln_b, win, qg, wuq, wiq, kvg, wuk, wuvt, ig2, ib2, mu):
    bsz, tp, _ = xcat.shape
    tr = _row_tile(tp)
    tok = lambda w: pl.BlockSpec((1, tr, w), lambda b, j: (b, j, 0))
    chan = lambda c: pl.BlockSpec((1, c, tr), lambda b, j: (b, 0, j))
    consts = [ln_g, ln_b, win, qg, wuq, wiq, kvg, wuk, wuvt, ig2, ib2, mu]
    return pl.pallas_call(
        functools.partial(_inproj_kernel, tr=tr),
        grid=(bsz, tp // tr),
        in_specs=[tok(D_MODEL)] + [_const_spec(c.shape) for c in consts],
        out_specs=[tok(A_WIDTH), tok(A_WIDTH), chan(IDX_HEADS), tok(A_WIDTH), chan(A_WIDTH),
                   tok(LANES), tok(B_COLS)],
        out_shape=[jax.ShapeDtypeStruct((bsz, tp, A_WIDTH), BF16),
                   jax.ShapeDtypeStruct((bsz, tp, A_WIDTH), BF16),
                   jax.ShapeDtypeStruct((bsz, IDX_HEADS, tp), F32),
                   jax.ShapeDtypeStruct((bsz, tp, A_WIDTH), BF16),
                   jax.ShapeDtypeStruct((bsz, A_WIDTH, tp), BF16),
                   jax.ShapeDtypeStruct((bsz, tp, LANES), BF16),
                   jax.ShapeDtypeStruct((bsz, tp, B_COLS), F32)],
        scratch_shapes=[pltpu.VMEM((8, B_COLS), F32)],
        compiler_params=pltpu.CompilerParams(
            dimension_semantics=("arbitrary", "arbitrary"), vmem_limit_bytes=VMEM_LIMIT),
        name="inproj",
    )(xcat, *consts)


def _chunk_of(pos):
    return jnp.maximum((pos - OFF + CHUNK) >> 6, 0)


def _pair_blockdiag(slab, lo_half):
    zero = jnp.zeros_like(slab)
    return jnp.concatenate([jnp.where(lo_half, slab, zero), jnp.where(lo_half, zero, slab)], axis=0)


def _attn_kernel(q_ref, qi_ref, w_ref, kx_ref, k_ref, vt_ref, featk_ref, featq_ref, cq_ref, sdd_ref,
                 vfirst_ref, vdiag_ref,
                 o_ref, s_ref, m_ref, l_ref, acc_ref, st_ref, bias_ref, tcar_ref, rmax_ref, rmin_ref,
                 *, topk):
    i = pl.program_id(1)
    nkt = i + 1
    lo_lane = lax.broadcasted_iota(jnp.int32, (TQ, LANES), 1) < A_HEAD_DIM
    lo_row = lax.broadcasted_iota(jnp.int32, (LANES, TQ), 0) < A_HEAD_DIM
    qp = i * TQ + lax.broadcasted_iota(jnp.int32, (1, TQ), 1)
    qc = _chunk_of(qp)
    kf = float(topk)

    qi = qi_ref[0]
    w = w_ref[0]

    def score_tile(kt, carry):
        rmax, rmin = carry
        base = pl.multiple_of(kt * TQ, TQ)
        z_idx = _pair_blockdiag(kx_ref[0, pl.ds(base, TQ), :], lo_lane)
        acc = jnp.zeros((TQ, TQ), F32)
        for jp in range(IDX_HEADS // 2):
            zz = _dot_nt(z_idx, qi[:, jp * LANES:(jp + 1) * LANES])
            r = jnp.maximum(zz, 0.0)
            acc = acc + r[:TQ] * w[2 * jp:2 * jp + 1] + r[TQ:] * w[2 * jp + 1:2 * jp + 2]
        s_ref[pl.ds(base, TQ), :] = acc
        rmax = jnp.maximum(rmax, jnp.max(acc, axis=0, keepdims=True))
        rmin = jnp.minimum(rmin, jnp.min(acc, axis=0, keepdims=True))
        return rmax, rmin

    ext = score_tile(0, (jnp.full((1, TQ), -jnp.inf, F32), jnp.full((1, TQ), jnp.inf, F32)))
    ext = lax.fori_loop(0, i // 2, lambda g, c: score_tile(2 * g + 2, score_tile(2 * g + 1, c)), ext)
    rmax_ref[...] = ext[0]
    rmin_ref[...] = ext[1]

    @pl.when((i & 1) == 1)
    def _():
        last = score_tile(i, (rmax_ref[...], rmin_ref[...]))
        rmax_ref[...] = last[0]
        rmin_ref[...] = last[1]

    rmax = rmax_ref[...]
    rmin = rmin_ref[...]
    s_ref[0:TQ, :] = s_ref[0:TQ, :] + vfirst_ref[...]

    @pl.when(i > 0)
    def _():
        dbase = pl.multiple_of(i * TQ, TQ)
        s_ref[pl.ds(dbase, TQ), :] = s_ref[pl.ds(dbase, TQ), :] + vdiag_ref[...]

    s_ref[pl.ds(pl.multiple_of(nkt * TQ, TQ), TQ), :] = jnp.full((TQ, TQ), -jnp.inf, F32)

    def count_above(thr, strict=True):
        def cnt_body(g, part):
            for half in range(2):
                base = pl.multiple_of(g * (2 * TQ) + half * TQ, TQ)
                for r0 in range(0, TQ, 64):
                    s = s_ref[pl.ds(base + r0, 64), :]
                    hit = jnp.where(s > thr if strict else s >= thr, 1.0, 0.0)
                    part = part + jnp.sum(hit.reshape(8, 8, TQ), axis=0)
            return part

        part = lax.fori_loop(0, (nkt + 1) // 2, cnt_body, jnp.zeros((8, TQ), F32))
        return jnp.sum(part, axis=0, keepdims=True)

    nvis = (N_META + CHUNK * qc).astype(F32)
    zero = jnp.zeros((1, TQ), F32)
    c_gt0 = count_above(zero)
    c_ge0 = count_above(zero, strict=False)
    all_visible = nvis <= kf
    at_zero = jnp.logical_and(c_gt0 <= kf, c_ge0 >= kf)
    positive = c_gt0 > kf
    below_min = rmin - jnp.maximum(jnp.abs(rmin) * (2.0 ** -20), 1e-30)
    st_ref[0:1, :] = jnp.where(positive, zero, below_min)
    st_ref[1:2, :] = jnp.where(positive, rmax, zero)
    st_ref[3:4, :] = jnp.where(positive, zero, c_gt0)
    st_ref[2:3, :] = jnp.where(positive, c_gt0, nvis) - kf
    st_ref[7:8, :] = kf - jnp.where(positive, zero, c_gt0)
    st_ref[8:9, :] = zero
    st_ref[4:5, :] = jnp.where(all_visible, -jnp.inf, zero)
    st_ref[5:6, :] = jnp.where(jnp.logical_and(at_zero, jnp.logical_not(all_visible)), kf - c_gt0, zero)
    st_ref[6:7, :] = jnp.where(jnp.logical_or(all_visible, at_zero), 1.0, 0.0)

    def search_cond(c):
        it, active = c
        return jnp.logical_and(active > 0.0, it < 400)

    def search_body(c):
        it, _ = c
        lo = st_ref[0:1, :]
        hi = st_ref[1:2, :]
        chi = st_ref[3:4, :]
        f_lo = st_ref[2:3, :]
        g_hi = st_ref[7:8, :]
        side = st_ref[8:9, :]
        thr = st_ref[4:5, :]
        need = st_ref[5:6, :]
        done = st_ref[6:7, :]
        sign = jnp.where(hi > 0.0, 1.0, -1.0)
        small = jnp.minimum(jnp.abs(lo), jnp.abs(hi))
        large = jnp.maximum(jnp.abs(lo), jnp.abs(hi))
        geo = sign * jnp.where(small == 0.0, large * (2.0 ** -12), jnp.sqrt(small * large))
        halve = jnp.where(large > 4.0 * small, geo, lo + (hi - lo) * 0.5)
        lin = lo + (hi - lo) * jnp.clip(f_lo / (f_lo + g_hi), 0.002, 0.998)
        use_lin = jnp.logical_and(jnp.logical_and(lin > lo, lin < hi), (it & 3) != 3)
        mid = jnp.where(use_lin, lin, halve)
        inside = jnp.logical_and(mid > lo, mid < hi)
        cnt = count_above(mid)
        nd = done < 0.5
        nd_in = jnp.logical_and(nd, inside)
        exact = jnp.logical_and(nd_in, cnt == kf)
        coll = jnp.logical_and(nd, jnp.logical_not(inside))
        go_lo = jnp.logical_and(nd_in, cnt > kf)
        go_hi = jnp.logical_and(nd_in, cnt < kf)
        st_ref[4:5, :] = jnp.where(exact, mid, jnp.where(coll, hi, thr))
        st_ref[5:6, :] = jnp.where(coll, kf - chi, need)
        done = jnp.where(jnp.logical_or(exact, coll), 1.0, done)
        st_ref[6:7, :] = done
        st_ref[0:1, :] = jnp.where(go_lo, mid, lo)
        st_ref[1:2, :] = jnp.where(go_hi, mid, hi)
        st_ref[3:4, :] = jnp.where(go_hi, cnt, chi)
        st_ref[2:3, :] = jnp.where(go_lo, cnt - kf,
                                   jnp.where(jnp.logical_and(go_hi, side < 0.0), f_lo * 0.5, f_lo))
        st_ref[7:8, :] = jnp.where(go_hi, kf - cnt,
                                   jnp.where(jnp.logical_and(go_lo, side > 0.0), g_hi * 0.5, g_hi))
        st_ref[8:9, :] = jnp.where(go_lo, 1.0, jnp.where(go_hi, -1.0, side))
        return it + 1, jnp.max(1.0 - done)

    lax.while_loop(search_cond, search_body, (jnp.int32(0), jnp.max(1.0 - st_ref[6:7, :])))
    unfinished = st_ref[6:7, :] < 0.5
    thr = jnp.where(unfinished, st_ref[1:2, :], st_ref[4:5, :])
    need = jnp.where(unfinished, kf - st_ref[3:4, :], st_ref[5:6, :])
    tie_any = jnp.max(need) > 0.0

    m_ref[...] = jnp.full(m_ref.shape, NEG, F32)
    l_ref[...] = jnp.zeros_like(l_ref)
    acc_ref[...] = jnp.zeros_like(acc_ref)
    tcar_ref[...] = jnp.zeros_like(tcar_ref)
    q = q_ref[0]

    def select_bias(kt, slot):
        base = pl.multiple_of(kt * TQ, TQ)
        s = s_ref[pl.ds(base, TQ), :]
        bias_ref[slot] = jnp.where(s > thr, 0.0, NEG)

        @pl.when(tie_any)
        def _():
            tie = jnp.logical_and(s == thr, need > 0.0)
            tie_b = jnp.where(tie, 1.0, 0.0).astype(BF16)
            lower = (lax.broadcasted_iota(jnp.int32, (TQ, TQ), 1)
                     <= lax.broadcasted_iota(jnp.int32, (TQ, TQ), 0))
            prefix = _dot(jnp.where(lower, 1.0, 0.0).astype(BF16), tie_b)
            rank = tcar_ref[...] + prefix
            take = jnp.logical_and(tie, rank <= need)
            bias_ref[slot] = jnp.where(jnp.logical_or(s > thr, take), 0.0, NEG)
            tcar_ref[...] = tcar_ref[...] + prefix[TQ - 1:TQ, :]

    def att_tiles(kts, diag):
        for slot, kt in enumerate(kts):
            select_bias(kt, slot)
        bases = [pl.multiple_of(kt * TQ, TQ) for kt in kts]
        kks = [k_ref[0, pl.ds(base, TQ), :] for base in bases]
        gaps = [((i - kt) * TQ).astype(F32) for kt in kts]
        lgs = []
        for jp in range(A_HEADS // 2):
            cols = slice(jp * LANES, (jp + 1) * LANES)
            if diag:
                zk = _pair_blockdiag(kks[0][:, cols], lo_lane)
                lgs.append(_dot_nt(zk, q[:, cols]))
            else:
                zk = jnp.concatenate(
                    [jnp.concatenate([_pair_blockdiag(kk[:, cols], lo_lane), featk_ref[...]], axis=1)
                     for kk in kks], axis=0)
                lgs.append(_dot_nt(zk, jnp.concatenate([q[:, cols], featq_ref[jp]], axis=1)))
        for jp in range(A_HEADS // 2):
            cols = slice(jp * LANES, (jp + 1) * LANES)
            lg = lgs[jp]
            ps, alphas = [[None, None] for _ in kts], []
            for hh in range(2):
                h = 2 * jp + hh
                lghs, shifts = [], []
                for t in range(len(kts)):
                    r0 = (2 * t + hh) * TQ
                    if diag:
                        lghs.append(lg[r0:r0 + TQ] - sdd_ref[h] + bias_ref[t])
                        shifts.append(0.0)
                    else:
                        lghs.append(lg[r0:r0 + TQ] + bias_ref[t])
                        shifts.append((LOG2E * 2.0 ** -(h + 1)) * gaps[t] + cq_ref[h:h + 1, :])
                m_old = m_ref[h:h + 1, :]
                m_new = m_old
                for lgh, shift in zip(lghs, shifts):
                    m_new = jnp.maximum(m_new, jnp.max(lgh, axis=0, keepdims=True) - shift)
                alpha = jnp.exp2(m_old - m_new)
                l_new = alpha * l_ref[h:h + 1, :]
                for t, (lgh, shift) in enumerate(zip(lghs, shifts)):
                    p = jnp.exp2(lgh - (m_new + shift))
                    l_new = l_new + jnp.sum(p, axis=0, keepdims=True)
                    ps[t][hh] = p.astype(BF16)
                l_ref[h:h + 1, :] = l_new
                m_ref[h:h + 1, :] = m_new
                alphas.append(alpha)
            zvts = []
            for base in bases:
                vt = vt_ref[0, cols, pl.ds(base, TQ)]
                zero = jnp.zeros_like(vt)
                zvts += [jnp.where(lo_row, vt, zero), jnp.where(lo_row, zero, vt)]
            pv = _dot(jnp.concatenate(zvts, axis=1),
                      jnp.concatenate([p for pt in ps for p in pt], axis=0))
            acc_ref[jp] = acc_ref[jp] * jnp.where(lo_row, alphas[0], alphas[1]) + pv

    def att_body(g, _):
        att_tiles([2 * g, 2 * g + 1], False)
        return 0

    lax.fori_loop(0, i // 2, att_body, 0)

    @pl.when((i & 1) == 1)
    def _():
        att_tiles([i - 1], False)

    att_tiles([i], True)
    for jp in range(A_HEADS // 2):
        denom = jnp.where(lo_row, l_ref[2 * jp:2 * jp + 1, :], l_ref[2 * jp + 1:2 * jp + 2, :])
        o_ref[0, :, jp * LANES:(jp + 1) * LANES] = (acc_ref[jp] / denom).T.astype(BF16)


def _attention(q, qi, wi, kx, k, vt, topk):
    bsz, tp, _ = q.shape
    qtile = lambda w: pl.BlockSpec((1, TQ, w), lambda b, i: (b, i, 0))
    whole = lambda r, c: pl.BlockSpec((1, r, c), lambda b, i: (b, 0, 0), pipeline_mode=pl.Buffered(1))
    kq = jnp.arange(TQ)
    slopes = (LOG2E * 2.0 ** -(jnp.arange(A_HEADS) + 1.0)).astype(F32)
    s0 = slopes.astype(BF16)
    s1 = (slopes - s0.astype(F32)).astype(BF16)
    s2 = (slopes - s0.astype(F32) - s1.astype(F32)).astype(BF16)
    parts = jnp.stack([s0, s1, s2], axis=1)
    col = jnp.arange(LANES)
    kloc = kq.astype(BF16)[:, None]
    featk = jnp.concatenate([jnp.where(col[None, :] < 3, kloc, 0),
                             jnp.where((col[None, :] >= 3) & (col[None, :] < 6), kloc, 0)], axis=0)
    featq_rows = jnp.zeros((A_HEADS // 2, LANES), BF16)
    featq_rows = featq_rows.at[:, 0:3].set(parts[0::2]).at[:, 3:6].set(parts[1::2])
    featq = jnp.broadcast_to(featq_rows[:, None, :], (A_HEADS // 2, TQ, LANES))
    cq = slopes[:, None] * kq[None, :].astype(F32)
    rel = (kq[None, :] - kq[:, None]).astype(F32)
    sdd = slopes[:, None, None] * jnp.abs(rel)[None]
    vfirst = jnp.where(kq[:, None] >= PAD, 0.0, -jnp.inf) * jnp.ones((1, TQ), F32)
    vdiag = jnp.where((kq[:, None] // CHUNK) <= (kq[None, :] // CHUNK), 0.0, -jnp.inf).astype(F32)
    consts = [featk.astype(BF16), featq, cq, sdd, vfirst.astype(F32), vdiag]
    return pl.pallas_call(
        functools.partial(_attn_kernel, topk=topk),
        grid=(bsz, tp // TQ),
        in_specs=[qtile(A_WIDTH), qtile(A_WIDTH),
                  pl.BlockSpec((1, IDX_HEADS, TQ), lambda b, i: (b, 0, i)),
                  whole(tp, LANES), whole(tp, A_WIDTH), whole(A_WIDTH, tp)]
        + [pl.BlockSpec(c.shape, lambda b, i, nd=c.ndim: (0,) * nd, pipeline_mode=pl.Buffered(1))
           for c in consts],
        out_specs=qtile(A_WIDTH),
        out_shape=jax.ShapeDtypeStruct((bsz, tp, A_WIDTH), BF16),
        scratch_shapes=[pltpu.VMEM((tp + TQ, TQ), F32),
                        pltpu.VMEM((A_HEADS, TQ), F32),
                        pltpu.VMEM((A_HEADS, TQ), F32),
                        pltpu.VMEM((A_HEADS // 2, LANES, TQ), F32),
                        pltpu.VMEM((16, TQ), F32),
                        pltpu.VMEM((2, TQ, TQ), F32),
                        pltpu.VMEM((1, TQ), F32),
                        pltpu.VMEM((1, TQ), F32),
                        pltpu.VMEM((1, TQ), F32)],
        compiler_params=pltpu.CompilerParams(
            dimension_semantics=("arbitrary", "arbitrary"), vmem_limit_bytes=VMEM_LIMIT),
        name="attention",
    )(q, qi, wi, kx, k, vt, *consts)


def _split_dot(x, w):
    hi = x.astype(BF16)
    lo = (x - hi.astype(F32)).astype(BF16)
    return _dot(hi, w) + _dot(lo, w)


def _softplus(x):
    return jnp.maximum(x, 0.0) + jnp.log1p(jnp.exp(-jnp.abs(x)))


def _sigmoid(x):
    return 1.0 / (1.0 + jnp.exp(-x))


def _rwkv_kernel(pb_ref, w0_ref, wda_ref, a0_ref, wg_ref, kk_ref, ka_ref, rk_ref, gng_ref, gnb_ref,
                 ones_ref, bdm_ref, msl_ref, mil_ref, tri_ref,
                 o_ref, state_ref, lw_s, r_s, kn_s, ba_s, km_s, v_s, y_s, *, tr):
    j = pl.program_id(1)

    @pl.when(j == 0)
    def _():
        state_ref[...] = jnp.zeros_like(state_ref)

    pb = pb_ref[0]
    r = pb[:, 0:B_WIDTH]
    k = pb[:, B_WIDTH:2 * B_WIDTH]
    v = pb[:, 2 * B_WIDTH:3 * B_WIDTH]
    da = pb[:, 3 * B_WIDTH:3 * B_WIDTH + LANES]
    is_decay = lax.broadcasted_iota(jnp.int32, (tr, LANES), 1) < DECAY_LORA
    pre = _dot(jnp.where(is_decay, jnp.tanh(da), da).astype(BF16), wda_ref[...])
    w_log = -_softplus(-(w0_ref[...] + pre[:, :B_WIDTH])) - 0.5
    a = _sigmoid(a0_ref[...] + pre[:, B_WIDTH:])
    kk = k * kk_ref[...]
    ones_bd = ones_ref[...]
    norm = jnp.sqrt(_split_dot(kk * kk, ones_bd))
    kn = kk / jnp.maximum(norm, 1e-12)
    km = k * (1.0 + (a - 1.0) * ka_ref[...])
    lw_s[...] = -jnp.exp(w_log)
    r_s[...] = r
    kn_s[...] = kn
    ba_s[...] = kn * a
    km_s[...] = km
    v_s[...] = v

    bdm = bdm_ref[...]
    bdm_b = bdm.astype(BF16)
    msl = msl_ref[...]
    mil = mil_ref[...]
    eye_w = mil - msl
    tri = tri_ref[...]

    def bd4(x):
        return jnp.concatenate([x, x, x, x], axis=0) * bdm_b

    groups = range(B_WIDTH // GROUP)
    sls = [slice(g * GROUP, (g + 1) * GROUP) for g in groups]

    def free_stages(c):
        d = {}

        def prep():
            rows = pl.ds(pl.multiple_of(c * CHUNK, CHUNK), CHUNK)
            lw = lw_s[rows, :]
            cum = _split_dot_left(tri, lw)
            cum_end = cum[CHUNK - 1:CHUNK, :]
            e_neg = jnp.exp(-cum)
            e_end = jnp.exp(cum_end - cum)
            g_end = jnp.exp(cum_end)
            kn_c = kn_s[rows, :]
            ba_c = ba_s[rows, :]
            km_c = km_s[rows, :]
            vv = v_s[rows, :]
            a_t = -kn_c * jnp.exp(cum - lw)
            b_t = ba_c * e_neg
            k_t = km_c * e_neg
            r_t = r_s[rows, :] * jnp.exp(cum)
            b_h = ba_c * e_end
            k_h = km_c * e_end
            d["rows"] = rows
            d["a_b"] = [a_t[:, sl].astype(BF16) for sl in sls]
            d["r_b"] = [r_t[:, sl].astype(BF16) for sl in sls]
            d["b_bd"] = [bd4(b_t[:, sl].astype(BF16)) for sl in sls]
            d["k_bd"] = [bd4(k_t[:, sl].astype(BF16)) for sl in sls]
            d["v_b"] = [vv[:, sl].astype(BF16) for sl in sls]
            d["v_bd"] = [bd4(v) for v in d["v_b"]]
            d["bk"] = [jnp.concatenate([b_h[:, sl], k_h[:, sl]], axis=0).astype(BF16) for sl in sls]
            d["g_col"] = [jnp.transpose(jnp.broadcast_to(g_end[:, sl], (8, GROUP)))[:, 0:1] for sl in sls]

        def gram():
            ar = [jnp.concatenate([d["a_b"][g], d["r_b"][g]], axis=0) for g in groups]
            d["gb"] = [_dot_nt(ar[g], d["b_bd"][g]) for g in groups]
            d["gk"] = [_dot_nt(ar[g], d["k_bd"][g]) for g in groups]

        def masks():
            l_ab = [d["gb"][g][:CHUNK] * msl for g in groups]
            l_ak = [(d["gk"][g][:CHUNK] * msl).astype(BF16) for g in groups]
            g_rk = [(d["gk"][g][CHUNK:] * mil).astype(BF16) for g in groups]
            d["g_rb"] = [(d["gb"][g][CHUNK:] * mil).astype(BF16) for g in groups]
            d["lv"] = [_dot(l_ak[g], d["v_bd"][g]) for g in groups]
            d["yv"] = [_dot(g_rk[g], d["v_bd"][g]) for g in groups]
            d["x"] = [eye_w + l_ab[g] for g in groups]
            d["pw"] = l_ab

        def level():
            pwb = [d["pw"][g].astype(BF16) for g in groups]
            d["pw"] = [_dot(pwb[g], bd4(pwb[g])) for g in groups]
            d["x"] = [d["x"][g] + _dot(d["x"][g].astype(BF16), bd4(d["pw"][g].astype(BF16))) for g in groups]

        return d, [prep, gram, masks] + [level] * 5

    def state_stages(d, st):
        o = {}

        def enter():
            st_b = [st[g].astype(BF16) for g in groups]
            o["u_in"] = [_dot(d["a_b"][g], st_b[g]) + d["lv"][g] for g in groups]
            o["y0"] = [_dot(d["r_b"][g], st_b[g]) + d["yv"][g] for g in groups]

        def solve():
            o["u_b"] = [_dot(d["x"][g].astype(BF16), bd4(o["u_in"][g].astype(BF16))).astype(BF16)
                        for g in groups]

        def leave():
            o["st"] = []
            for g in groups:
                y_s[d["rows"], sls[g]] = o["y0"][g] + _dot(d["g_rb"][g], bd4(o["u_b"][g]))
                uv = jnp.concatenate([o["u_b"][g], d["v_b"][g]], axis=0)
                o["st"].append(st[g] * d["g_col"][g] + _dot_tn(d["bk"][g], uv) * bdm)

        return o, [enter, solve, leave]

    def chunk_pair(cp, _):
        d0, free0 = free_stages(2 * cp)
        for stage in free0:
            stage()
        d1, free1 = free_stages(2 * cp + 1)
        o0, state0 = state_stages(d0, [state_ref[g] for g in groups])
        slots = {0: state0[0], 2: state0[1], 4: state0[2]}
        for n, stage in enumerate(free1):
            if n in slots:
                slots[n]()
            stage()
        o1, state1 = state_stages(d1, o0["st"])
        for stage in state1:
            stage()
        for g in groups:
            state_ref[g] = o1["st"][g]
        return 0

    lax.fori_loop(0, tr // (2 * CHUNK), chunk_pair, 0)

    y = y_s[...]
    inv = 1.0 / B_HEAD_DIM
    mu = _split_dot(y, ones_bd) * inv
    yc = y - mu
    var = _split_dot(yc * yc, ones_bd) * inv
    yn = yc * lax.rsqrt(var + GN_EPS) * gng_ref[...] + gnb_ref[...]
    bonus = _split_dot(r_s[...] * km_s[...] * rk_ref[...], ones_bd) * v_s[...]
    gd = pb_ref[0][:, 3 * B_WIDTH + LANES:]
    gate = _dot(_sigmoid(gd).astype(BF16), wg_ref[...])
    o_ref[0] = ((yn + bonus) * gate).astype(BF16)


def _split_dot_left(w, x):
    hi = x.astype(BF16)
    lo = (x - hi.astype(F32)).astype(BF16)
    return _dot(w, hi) + _dot(w, lo)


def _rwkv(pbs, w0, wda, a0, wg, k_k, k_a, r_k, gn_g, gn_b, ones_bd, bdm, msl, mil, tri):
    bsz, tp, _ = pbs.shape
    tr = _row_tile(tp)
    consts = [w0, wda, a0, wg, k_k, k_a, r_k, gn_g, gn_b, ones_bd, bdm, msl, mil, tri]
    tok = lambda w: pl.BlockSpec((1, tr, w), lambda b, j: (b, j, 0))
    return pl.pallas_call(
        functools.partial(_rwkv_kernel, tr=tr),
        grid=(bsz, tp // tr),
        in_specs=[tok(B_COLS)] + [_const_spec(c.shape) for c in consts],
        out_specs=tok(B_WIDTH),
        out_shape=jax.ShapeDtypeStruct((bsz, tp, B_WIDTH), BF16),
        scratch_shapes=[pltpu.VMEM((B_WIDTH // GROUP, GROUP, GROUP), F32)]
        + [pltpu.VMEM((tr, B_WIDTH), F32)] * 7,
        compiler_params=pltpu.CompilerParams(
            dimension_semantics=("arbitrary", "arbitrary"), vmem_limit_bytes=VMEM_LIMIT),
        name="rwkv",
    )(pbs, *consts)


def _gelu_tanh(x):
    return 0.5 * x * (1.0 + jnp.tanh(0.7978845608028654 * (x + 0.044715 * (x * x * x))))


def _ffn_kernel(x_ref, ya_ref, yb_ref, lng_ref, lnb_ref, wout_ref, l1g_ref, l1b_ref, wup_ref,
                cw_ref, cb_ref, wdn_ref, l2g_ref, l2b_ref, o_ref, hist_ref, act_ref, *, tr):
    j = pl.program_id(1)

    @pl.when(j == 0)
    def _():
        hist_ref[...] = jnp.zeros_like(hist_ref)

    h0 = _layer_norm(x_ref[0], lng_ref[...], lnb_ref[...])
    mix = _dot(ya_ref[0], wout_ref[:A_WIDTH, :]) + _dot(yb_ref[0], wout_ref[A_WIDTH:, :])
    h1 = _layer_norm(ALPHA * h0 + mix, l1g_ref[...], l1b_ref[...])
    h1b = h1.astype(BF16)
    row = lax.broadcasted_iota(jnp.int32, (tr, 1), 0)
    valid = (j * tr + row) >= PAD

    def up(c):
        return (_dot(h1b, wup_ref[:, c * FF_CHUNK:(c + 1) * FF_CHUNK]),
                _dot(h1b, wup_ref[:, D_FF + c * FF_CHUNK:D_FF + (c + 1) * FF_CHUNK]))

    nxt = up(0)
    for c in range(N_FF_CHUNKS):
        cs = slice(c * FF_CHUNK, (c + 1) * FF_CHUNK)
        gate, val = nxt
        if c + 1 < N_FF_CHUNKS:
            nxt = up(c + 1)
        gate = jnp.where(valid, gate, 0.0)
        hist = hist_ref[c]
        g1 = jnp.where(row == 0, hist[7:8, :], pltpu.roll(gate, 1, 0))
        g2 = jnp.where(row == 0, hist[6:7, :], jnp.where(row == 1, hist[7:8, :], pltpu.roll(gate, 2, 0)))
        hist_ref[c] = gate[tr - 8:, :]
        conv = cw_ref[0:1, cs] * g2 + cw_ref[1:2, cs] * g1 + cw_ref[2:3, cs] * gate + cb_ref[:, cs]
        act_ref[:, cs] = (_gelu_tanh(conv) * val).astype(BF16)
    ffn = _dot(act_ref[...], wdn_ref[...])
    o_ref[0] = _layer_norm(ALPHA * h1 + ffn, l2g_ref[...], l2b_ref[...])


def _ffn(xcat, ya, yb, ln_g, ln_b, wout, l1g, l1b, wup, cw, cb, wdn, l2g, l2b):
    bsz, tp, _ = xcat.shape
    tr = _row_tile(tp)
    tok = lambda w: pl.BlockSpec((1, tr, w), lambda b, j: (b, j, 0))

    def res(c):
        nd = c.ndim
        return pl.BlockSpec(c.shape, lambda *_: (0,) * nd, pipeline_mode=pl.Buffered(1))

    return pl.pallas_call(
        functools.partial(_ffn_kernel, tr=tr),
        grid=(bsz, tp // tr),
        in_specs=[tok(D_MODEL), tok(A_WIDTH), tok(B_WIDTH), res(ln_g), res(ln_b), res(wout), res(l1g),
                  res(l1b), res(wup), res(cw), res(cb), res(wdn), res(l2g), res(l2b)],
        out_specs=tok(D_MODEL),
        out_shape=jax.ShapeDtypeStruct((bsz, tp, D_MODEL), F32),
        scratch_shapes=[pltpu.VMEM((N_FF_CHUNKS, 8, FF_CHUNK), F32), pltpu.VMEM((tr, D_FF), BF16)],
        compiler_params=pltpu.CompilerParams(
            dimension_semantics=("arbitrary", "arbitrary"), vmem_limit_bytes=VMEM_LIMIT),
        name="ffn",
    )(xcat, ya, yb, ln_g, ln_b, wout, l1g, l1b, wup, cw, cb, wdn, l2g, l2b)


def _row(v):
    return v.reshape(1, -1).astype(F32)


def kernel(x, meta, ln_in_g, ln_in_b, w_in, q_norm_g, w_uq, kv_norm_g, w_ukv, w_iq, idx_ln_g, idx_ln_b, mu_shift, w0, w_decay_up, a0, w_aaa_up, w_gate_up, k_k, k_a, r_k, gn_g, gn_b, w_out, ln1_g, ln1_b, w_ffn_up, conv_w, conv_b, w_ffn_down, ln2_g, ln2_b):
    bsz, seq, _ = x.shape
    assert seq % TQ == 0 and w_in.shape[0] == DEPTH
    t_real = N_META + seq
    topk = min(INDEX_TOPK, t_real // 4)

    xcat = jnp.concatenate([jnp.zeros((bsz, PAD, D_MODEL), x.dtype),
                            jnp.broadcast_to(meta[None].astype(x.dtype), (bsz, N_META, D_MODEL)), x], axis=1)

    wi = w_in[0]
    kidx_cols = wi[:, Q_RANK + KV_RANK:Q_RANK + KV_RANK + IDX_DIM]
    win = jnp.concatenate([
        wi[:, :Q_RANK + KV_RANK], kidx_cols, kidx_cols, wi[:, Q_RANK + KV_RANK + IDX_DIM:A_COLS],
        jnp.zeros((D_MODEL, A_PAD - 512 - IDX_HEADS), F32), wi[:, A_COLS:]], axis=1).astype(BF16)
    wukv = w_ukv[0].reshape(KV_RANK, A_HEADS, 2, A_HEAD_DIM)
    wuk = wukv[:, :, 0].reshape(KV_RANK, A_WIDTH).astype(BF16)
    wuvt = wukv[:, :, 1].reshape(KV_RANK, A_WIDTH).T.astype(BF16)
    ig2 = _row(jnp.concatenate([idx_ln_g[0], idx_ln_g[0]]))
    ib2 = _row(jnp.concatenate([idx_ln_b[0], idx_ln_b[0]]))

    q, qi, widx, k, vt, kx, pbs = _inproj(
        xcat, _row(ln_in_g), _row(ln_in_b), win, _row(q_norm_g[0]), w_uq[0].astype(BF16),
        w_iq[0].astype(BF16), _row(kv_norm_g[0]), wuk, wuvt, ig2, ib2, _row(mu_shift[0]))

    ya = _attention(q, qi, widx, kx, k, vt, topk)

    zl = jnp.zeros((DECAY_LORA, B_WIDTH), F32)
    wda = jnp.concatenate([jnp.concatenate([w_decay_up[0], zl], axis=1),
                           jnp.concatenate([zl, w_aaa_up[0]], axis=1)], axis=0).astype(BF16)
    ci = jnp.arange(B_WIDTH)
    ones_bd = (ci[:, None] // B_HEAD_DIM == ci[None, :] // B_HEAD_DIM).astype(BF16)
    gi = jnp.arange(GROUP)
    bdm = (gi[:, None] // B_HEAD_DIM == gi[None, :] // B_HEAD_DIM).astype(F32)
    ti = jnp.arange(CHUNK)
    msl = (ti[:, None] > (gi[None, :] % CHUNK)).astype(F32)
    mil = (ti[:, None] >= (gi[None, :] % CHUNK)).astype(F32)
    tri = (ti[:, None] >= ti[None, :]).astype(BF16)
    yb = _rwkv(pbs, _row(w0[0]), wda, _row(a0[0]), w_gate_up[0].astype(BF16), _row(k_k[0]), _row(k_a[0]),
               _row(r_k[0]), _row(gn_g[0]), _row(gn_b[0]), ones_bd, bdm, msl, mil, tri)

    out = _ffn(xcat, ya, yb, _row(ln_in_g), _row(ln_in_b), w_out[0].astype(BF16), _row(ln1_g[0]),
               _row(ln1_b[0]), w_ffn_up[0].astype(BF16), conv_w[0].astype(F32), _row(conv_b[0]),
               w_ffn_down[0].astype(BF16), _row(ln2_g[0]), _row(ln2_b[0]))
    return out[:, OFF:]
```

```python
import functools

import jax
import jax.numpy as jnp
from jax import lax
from jax.experimental import pallas as pl
from jax.experimental.pallas import tpu as pltpu

F32 = jnp.float32
BF16 = jnp.bfloat16

D_MODEL = 1024
N_META = 16
CHUNK = 64
A_HEADS = 8
A_HEAD_DIM = 64
A_WIDTH = A_HEADS * A_HEAD_DIM
Q_RANK = 256
KV_RANK = 128
IDX_HEADS = 8
IDX_DIM = 64
INDEX_TOPK = 256
B_HEADS = 8
B_HEAD_DIM = 64
B_WIDTH = B_HEADS * B_HEAD_DIM
DECAY_LORA = 64
AAA_LORA = 64
GATE_LORA = 128
GN_EPS = 64e-5
D_FF = 2816
LN_EPS = 1e-5
DEPTH = 1
ALPHA = (2 * DEPTH) ** 0.25
A_COLS = Q_RANK + KV_RANK + IDX_DIM + IDX_HEADS
B_COLS = 3 * B_WIDTH + DECAY_LORA + AAA_LORA + GATE_LORA
A_PAD = 640
IN_PAD = A_PAD + B_COLS

LANES = 128
TQ = 256
OFF = TQ
PAD = OFF - N_META
FF_CHUNK = 256
N_FF_CHUNKS = D_FF // FF_CHUNK
GROUP = 256
NEG = -1e30
LOG2E = 1.4426950408889634
VMEM_LIMIT = 56 * 1024 * 1024


def _row_tile(tp):
    for cand in (768, 512, 256):
        if tp % cand == 0:
            return cand
    raise ValueError(tp)


def _layer_norm(x, g, b):
    mu = jnp.mean(x, -1, keepdims=True)
    xc = x - mu
    var = jnp.mean(xc * xc, -1, keepdims=True)
    return xc * lax.rsqrt(var + LN_EPS) * g + b


def _dot(a, b):
    return jnp.dot(a, b, preferred_element_type=F32)


def _dot_nt(a, b):
    return lax.dot_general(a, b, (((1,), (1,)), ((), ())), preferred_element_type=F32)


def _dot_tn(a, b):
    return lax.dot_general(a, b, (((0,), (0,)), ((), ())), preferred_element_type=F32)


def _const_spec(shape):
    nd = len(shape)
    return pl.BlockSpec(shape, lambda *_: (0,) * nd)


def _inproj_kernel(x_ref, lng_ref, lnb_ref, win_ref, qg_ref, wuq_ref, wiq_ref, kvg_ref, wuk_ref,
                   wuvt_ref, ig_ref, ib_ref, mu_ref,
                   q_ref, qi_ref, wi_ref, k_ref, vt_ref, kx_ref, pb_ref, carry_ref, *, tr):
    j = pl.program_id(1)

    @pl.when(j == 0)
    def _():
        carry_ref[...] = jnp.zeros_like(carry_ref)

    h = _layer_norm(x_ref[0], lng_ref[...], lnb_ref[...])
    pos = j * tr + lax.broadcasted_iota(jnp.int32, (tr, 1), 0)
    valid = pos >= PAD
    h = jnp.where(valid, h, 0.0)
    p = _dot(h.astype(BF16), win_ref[...])

    cq = p[:, :Q_RANK]
    cq = cq * lax.rsqrt(jnp.mean(cq * cq, -1, keepdims=True) + 1e-6) * qg_ref[...]
    cqb = cq.astype(BF16)
    q_ref[0] = (_dot(cqb, wuq_ref[...]) * (A_HEAD_DIM ** -0.5 * LOG2E)).astype(BF16)
    qi_ref[0] = _dot(cqb, wiq_ref[...]).astype(BF16)
    ckv = p[:, Q_RANK:Q_RANK + KV_RANK]
    ckv = ckv * lax.rsqrt(jnp.mean(ckv * ckv, -1, keepdims=True) + 1e-6) * kvg_ref[...]
    ckvb = ckv.astype(BF16)
    k_ref[0] = _dot(ckvb, wuk_ref[...]).astype(BF16)
    vt_ref[0] = _dot_nt(wuvt_ref[...], ckvb).astype(BF16)
    kx_ref[0] = _layer_norm(p[:, 384:512], ig_ref[...], ib_ref[...]).astype(BF16)
    wi_ref[0] = p[:, 512:A_PAD].T[:IDX_HEADS, :] * (IDX_HEADS ** -0.5 * IDX_DIM ** -0.5)

    pb = p[:, A_PAD:]
    prev = carry_ref[7:8, :]
    row = lax.broadcasted_iota(jnp.int32, (tr, 1), 0)
    shifted = jnp.where(row == 0, prev, pltpu.roll(pb, 1, 0))
    carry_ref[...] = pb[tr - 8:, :]
    pbs = pb + (shifted - pb) * mu_ref[...]
    pb_ref[0] = jnp.where(valid, pbs, 0.0)


def _inproj(xcat, ln_g, ln_b, win, qg, wuq, wiq, kvg, wuk, wuvt, ig2, ib2, mu):
    bsz, tp, _ = xcat.shape
    tr = _row_tile(tp)
    tok = lambda w: pl.BlockSpec((1, tr, w), lambda b, j: (b, j, 0))
    chan = lambda c: pl.BlockSpec((1, c, tr), lambda b, j: (b, 0, j))
    consts = [ln_g, ln_b, win, qg, wuq, wiq, kvg, wuk, wuvt, ig2, ib2, mu]
    return pl.pallas_call(
        functools.partial(_inproj_kernel, tr=tr),
        grid=(bsz, tp // tr),
        in_specs=[tok(D_MODEL)] + [_const_spec(c.shape) for c in consts],
        out_specs=[tok(A_WIDTH), tok(A_WIDTH), chan(IDX_HEADS), tok(A_WIDTH), chan(A_WIDTH),
                   tok(LANES), tok(B_COLS)],
        out_shape=[jax.ShapeDtypeStruct((bsz, tp, A_WIDTH), BF16),
                   jax.ShapeDtypeStruct((bsz, tp, A_WIDTH), BF16),
                   jax.ShapeDtypeStruct((bsz, IDX_HEADS, tp), F32),
                   jax.ShapeDtypeStruct((bsz, tp, A_WIDTH), BF16),
                   jax.ShapeDtypeStruct((bsz, A_WIDTH, tp), BF16),
                   jax.ShapeDtypeStruct((bsz, tp, LANES), BF16),
                   jax.ShapeDtypeStruct((bsz, tp, B_COLS), F32)],
        scratch_shapes=[pltpu.VMEM((8, B_COLS), F32)],
        compiler_params=pltpu.CompilerParams(
            dimension_semantics=("arbitrary", "arbitrary"), vmem_limit_bytes=VMEM_LIMIT),
        name="inproj",
    )(xcat, *consts)


def _chunk_of(pos):
    return jnp.maximum((pos - OFF + CHUNK) >> 6, 0)


def _pair_blockdiag(slab, lo_half):
    zero = jnp.zeros_like(slab)
    return jnp.concatenate([jnp.where(lo_half, slab, zero), jnp.where(lo_half, zero, slab)], axis=0)


def _attn_kernel(q_ref, qi_ref, w_ref, kx_ref, k_ref, vt_ref, featk_ref, featq_ref, cq_ref, sdd_ref,
                 vfirst_ref, vdiag_ref, ones_ref,
                 o_ref, s_ref, m_ref, l_ref, acc_ref, st_ref, bias_ref, tcar_ref, rmax_ref, rmin_ref,
                 *, topk):
    i = pl.program_id(1)
    nkt = i + 1
    lo_lane = lax.broadcasted_iota(jnp.int32, (TQ, LANES), 1) < A_HEAD_DIM
    lo_row = lax.broadcasted_iota(jnp.int32, (LANES, TQ), 0) < A_HEAD_DIM
    qp = i * TQ + lax.broadcasted_iota(jnp.int32, (1, TQ), 1)
    qc = _chunk_of(qp)
    kf = float(topk)

    qi = qi_ref[0]
    w = w_ref[0]

    def score_tiles(kts, carry):
        rmax, rmin = carry
        bases = [pl.multiple_of(kt * TQ, TQ) for kt in kts]
        z_idx = [_pair_blockdiag(kx_ref[0, pl.ds(base, TQ), :], lo_lane) for base in bases]
        zz = [[_dot_nt(z, qi[:, jp * LANES:(jp + 1) * LANES]) for jp in range(IDX_HEADS // 2)]
              for z in z_idx]
        for base, zt in zip(bases, zz):
            acc = jnp.zeros((TQ, TQ), F32)
            for jp, z in enumerate(zt):
                r = jnp.maximum(z, 0.0)
                acc = acc + r[:TQ] * w[2 * jp:2 * jp + 1] + r[TQ:] * w[2 * jp + 1:2 * jp + 2]
            s_ref[pl.ds(base, TQ), :] = acc
            rmax = jnp.maximum(rmax, jnp.max(acc, axis=0, keepdims=True))
            rmin = jnp.minimum(rmin, jnp.min(acc, axis=0, keepdims=True))
        return rmax, rmin

    ext = score_tiles([0], (jnp.full((1, TQ), -jnp.inf, F32), jnp.full((1, TQ), jnp.inf, F32)))
    ext = lax.fori_loop(0, i // 2, lambda g, c: score_tiles([2 * g + 1, 2 * g + 2], c), ext)
    rmax_ref[...] = ext[0]
    rmin_ref[...] = ext[1]

    @pl.when((i & 1) == 1)
    def _():
        last = score_tiles([i], (rmax_ref[...], rmin_ref[...]))
        rmax_ref[...] = last[0]
        rmin_ref[...] = last[1]

    rmax = rmax_ref[...]
    rmin = rmin_ref[...]
    s_ref[0:TQ, :] = s_ref[0:TQ, :] + vfirst_ref[...]

    @pl.when(i > 0)
    def _():
        dbase = pl.multiple_of(i * TQ, TQ)
        s_ref[pl.ds(dbase, TQ), :] = s_ref[pl.ds(dbase, TQ), :] + vdiag_ref[...]

    s_ref[pl.ds(pl.multiple_of(nkt * TQ, TQ), TQ), :] = jnp.full((TQ, TQ), -jnp.inf, F32)

    def count_above(thr, strict=True):
        def cnt_body(g, part):
            for half in range(2):
                base = pl.multiple_of(g * (2 * TQ) + half * TQ, TQ)
                for r0 in range(0, TQ, 64):
                    s = s_ref[pl.ds(base + r0, 64), :]
                    hit = jnp.where(s > thr if strict else s >= thr, 1.0, 0.0)
                    part = part + jnp.sum(hit.reshape(8, 8, TQ), axis=0)
            return part

        part = lax.fori_loop(0, (nkt + 1) // 2, cnt_body, jnp.zeros((8, TQ), F32))
        return jnp.sum(part, axis=0, keepdims=True)

    nvis = (N_META + CHUNK * qc).astype(F32)
    zero = jnp.zeros((1, TQ), F32)
    c_gt0 = count_above(zero)
    c_ge0 = count_above(zero, strict=False)
    all_visible = nvis <= kf
    at_zero = jnp.logical_and(c_gt0 <= kf, c_ge0 >= kf)
    positive = c_gt0 > kf
    below_min = rmin - jnp.maximum(jnp.abs(rmin) * (2.0 ** -20), 1e-30)
    st_ref[0:1, :] = jnp.where(positive, zero, below_min)
    st_ref[1:2, :] = jnp.where(positive, rmax, zero)
    st_ref[3:4, :] = jnp.where(positive, zero, c_gt0)
    st_ref[2:3, :] = jnp.where(positive, c_gt0, nvis) - kf
    st_ref[7:8, :] = kf - jnp.where(positive, zero, c_gt0)
    st_ref[8:9, :] = zero
    st_ref[4:5, :] = jnp.where(all_visible, -jnp.inf, zero)
    st_ref[5:6, :] = jnp.where(jnp.logical_and(at_zero, jnp.logical_not(all_visible)), kf - c_gt0, zero)
    st_ref[6:7, :] = jnp.where(jnp.logical_or(all_visible, at_zero), 1.0, 0.0)

    def search_cond(c):
        it, active = c
        return jnp.logical_and(active > 0.0, it < 400)

    def search_body(c):
        it, _ = c
        lo = st_ref[0:1, :]
        hi = st_ref[1:2, :]
        chi = st_ref[3:4, :]
        f_lo = st_ref[2:3, :]
        g_hi = st_ref[7:8, :]
        side = st_ref[8:9, :]
        thr = st_ref[4:5, :]
        need = st_ref[5:6, :]
        done = st_ref[6:7, :]
        sign = jnp.where(hi > 0.0, 1.0, -1.0)
        small = jnp.minimum(jnp.abs(lo), jnp.abs(hi))
        large = jnp.maximum(jnp.abs(lo), jnp.abs(hi))
        geo = sign * jnp.where(small == 0.0, large * (2.0 ** -12), jnp.sqrt(small * large))
        halve = jnp.where(large > 4.0 * small, geo, lo + (hi - lo) * 0.5)
        lin = lo + (hi - lo) * jnp.clip(f_lo / (f_lo + g_hi), 0.002, 0.998)
        use_lin = jnp.logical_and(jnp.logical_and(lin > lo, lin < hi), (it & 3) != 3)
        mid = jnp.where(use_lin, lin, halve)
        inside = jnp.logical_and(mid > lo, mid < hi)
        cnt = count_above(mid)
        nd = done < 0.5
        nd_in = jnp.logical_and(nd, inside)
        exact = jnp.logical_and(nd_in, cnt == kf)
        coll = jnp.logical_and(nd, jnp.logical_not(inside))
        go_lo = jnp.logical_and(nd_in, cnt > kf)
        go_hi = jnp.logical_and(nd_in, cnt < kf)
        st_ref[4:5, :] = jnp.where(exact, mid, jnp.where(coll, hi, thr))
        st_ref[5:6, :] = jnp.where(coll, kf - chi, need)
        done = jnp.where(jnp.logical_or(exact, coll), 1.0, done)
        st_ref[6:7, :] = done
        st_ref[0:1, :] = jnp.where(go_lo, mid, lo)
        st_ref[1:2, :] = jnp.where(go_hi, mid, hi)
        st_ref[3:4, :] = jnp.where(go_hi, cnt, chi)
        st_ref[2:3, :] = jnp.where(go_lo, cnt - kf,
                                   jnp.where(jnp.logical_and(go_hi, side < 0.0), f_lo * 0.5, f_lo))
        st_ref[7:8, :] = jnp.where(go_hi, kf - cnt,
                                   jnp.where(jnp.logical_and(go_lo, side > 0.0), g_hi * 0.5, g_hi))
        st_ref[8:9, :] = jnp.where(go_lo, 1.0, jnp.where(go_hi, -1.0, side))
        return it + 1, jnp.max(1.0 - done)

    lax.while_loop(search_cond, search_body, (jnp.int32(0), jnp.max(1.0 - st_ref[6:7, :])))
    unfinished = st_ref[6:7, :] < 0.5
    thr = jnp.where(unfinished, st_ref[1:2, :], st_ref[4:5, :])
    need = jnp.where(unfinished, kf - st_ref[3:4, :], st_ref[5:6, :])
    tie_any = jnp.max(need) > 0.0

    m_ref[...] = jnp.full(m_ref.shape, NEG, F32)
    l_ref[...] = jnp.zeros_like(l_ref)
    acc_ref[...] = jnp.zeros_like(acc_ref)
    tcar_ref[...] = jnp.zeros_like(tcar_ref)
    q = q_ref[0]

    def select_bias(kt, slot):
        base = pl.multiple_of(kt * TQ, TQ)
        s = s_ref[pl.ds(base, TQ), :]
        bias_ref[slot] = jnp.where(s > thr, 0.0, NEG)

        @pl.when(tie_any)
        def _():
            tie = jnp.logical_and(s == thr, need > 0.0)
            tie_b = jnp.where(tie, 1.0, 0.0).astype(BF16)
            lower = (lax.broadcasted_iota(jnp.int32, (TQ, TQ), 1)
                     <= lax.broadcasted_iota(jnp.int32, (TQ, TQ), 0))
            prefix = _dot(jnp.where(lower, 1.0, 0.0).astype(BF16), tie_b)
            rank = tcar_ref[...] + prefix
            take = jnp.logical_and(tie, rank <= need)
            bias_ref[slot] = jnp.where(jnp.logical_or(s > thr, take), 0.0, NEG)
            tcar_ref[...] = tcar_ref[...] + prefix[TQ - 1:TQ, :]

    def att_tiles(kts, diag):
        for slot, kt in enumerate(kts):
            select_bias(kt, slot)
        bases = [pl.multiple_of(kt * TQ, TQ) for kt in kts]
        kks = [k_ref[0, pl.ds(base, TQ), :] for base in bases]
        gaps = [((i - kt) * TQ).astype(F32) for kt in kts]
        lgs = []
        for jp in range(A_HEADS // 2):
            cols = slice(jp * LANES, (jp + 1) * LANES)
            if diag:
                zk = _pair_blockdiag(kks[0][:, cols], lo_lane)
                lgs.append(_dot_nt(zk, q[:, cols]))
            else:
                zk = jnp.concatenate(
                    [jnp.concatenate([_pair_blockdiag(kk[:, cols], lo_lane), featk_ref[...]], axis=1)
                     for kk in kks], axis=0)
                lgs.append(_dot_nt(zk, jnp.concatenate([q[:, cols], featq_ref[jp]], axis=1)))
        for jp in range(A_HEADS // 2):
            cols = slice(jp * LANES, (jp + 1) * LANES)
            lg = lgs[jp]
            ps, alphas = [[None, None] for _ in kts], []
            for hh in range(2):
                h = 2 * jp + hh
                lghs, shifts = [], []
                for t in range(len(kts)):
                    r0 = (2 * t + hh) * TQ
                    if diag:
                        lghs.append(lg[r0:r0 + TQ] - sdd_ref[h] + bias_ref[t])
                        shifts.append(0.0)
                    else:
                        lghs.append(lg[r0:r0 + TQ] + bias_ref[t])
                        shifts.append((LOG2E * 2.0 ** -(h + 1)) * gaps[t] + cq_ref[h:h + 1, :])
                m_old = m_ref[h:h + 1, :]
                m_new = m_old
                for lgh, shift in zip(lghs, shifts):
                    m_new = jnp.maximum(m_new, jnp.max(lgh, axis=0, keepdims=True) - shift)
                alpha = jnp.exp2(m_old - m_new)
                for t, (lgh, shift) in enumerate(zip(lghs, shifts)):
                    ps[t][hh] = jnp.exp2((lgh - (m_new + shift)).astype(BF16))
                m_ref[h:h + 1, :] = m_new
                alphas.append(alpha)
            zvts = []
            for base in bases:
                vt = vt_ref[0, cols, pl.ds(base, TQ)]
                zero = jnp.zeros_like(vt)
                zvts += [jnp.where(lo_row, vt, zero), jnp.where(lo_row, zero, vt)]
            lhs = jnp.concatenate([jnp.concatenate(zvts, axis=1),
                                   jnp.concatenate([ones_ref[...]] * len(kts), axis=1)], axis=0)
            pv = _dot(lhs, jnp.concatenate([p for pt in ps for p in pt], axis=0))
            acc_ref[jp] = acc_ref[jp] * jnp.where(lo_row, alphas[0], alphas[1]) + pv[:LANES]
            for hh in range(2):
                h = 2 * jp + hh
                l_ref[h:h + 1, :] = alphas[hh] * l_ref[h:h + 1, :] + pv[LANES + hh:LANES + hh + 1]

    def att_body(g, _):
        att_tiles([4 * g + t for t in range(4)], False)
        return 0

    lax.fori_loop(0, i // 4, att_body, 0)
    done4 = (i // 4) * 4

    @pl.when((i & 2) == 2)
    def _():
        att_tiles([done4, done4 + 1], False)

    @pl.when((i & 1) == 1)
    def _():
        att_tiles([i - 1], False)

    att_tiles([i], True)
    for jp in range(A_HEADS // 2):
        denom = jnp.where(lo_row, l_ref[2 * jp:2 * jp + 1, :], l_ref[2 * jp + 1:2 * jp + 2, :])
        o_ref[0, :, jp * LANES:(jp + 1) * LANES] = (acc_ref[jp] / denom).T.astype(BF16)


def _attention(q, qi, wi, kx, k, vt, topk):
    bsz, tp, _ = q.shape
    qtile = lambda w: pl.BlockSpec((1, TQ, w), lambda b, i: (b, i, 0))
    whole = lambda r, c: pl.BlockSpec((1, r, c), lambda b, i: (b, 0, 0), pipeline_mode=pl.Buffered(1))
    kq = jnp.arange(TQ)
    slopes = (LOG2E * 2.0 ** -(jnp.arange(A_HEADS) + 1.0)).astype(F32)
    s0 = slopes.astype(BF16)
    s1 = (slopes - s0.astype(F32)).astype(BF16)
    s2 = (slopes - s0.astype(F32) - s1.astype(F32)).astype(BF16)
    parts = jnp.stack([s0, s1, s2], axis=1)
    col = jnp.arange(LANES)
    kloc = kq.astype(BF16)[:, None]
    featk = jnp.concatenate([jnp.where(col[None, :] < 3, kloc, 0),
                             jnp.where((col[None, :] >= 3) & (col[None, :] < 6), kloc, 0)], axis=0)
    featq_rows = jnp.zeros((A_HEADS // 2, LANES), BF16)
    featq_rows = featq_rows.at[:, 0:3].set(parts[0::2]).at[:, 3:6].set(parts[1::2])
    featq = jnp.broadcast_to(featq_rows[:, None, :], (A_HEADS // 2, TQ, LANES))
    cq = slopes[:, None] * kq[None, :].astype(F32)
    rel = (kq[None, :] - kq[:, None]).astype(F32)
    sdd = slopes[:, None, None] * jnp.abs(rel)[None]
    vfirst = jnp.where(kq[:, None] >= PAD, 0.0, -jnp.inf) * jnp.ones((1, TQ), F32)
    vdiag = jnp.where((kq[:, None] // CHUNK) <= (kq[None, :] // CHUNK), 0.0, -jnp.inf).astype(F32)
    kcol = jnp.arange(2 * TQ)[None, :]
    orow = jnp.arange(16)[:, None]
    ones_rows = jnp.where(((orow == 0) & (kcol < TQ)) | ((orow == 1) & (kcol >= TQ)), 1.0, 0.0).astype(BF16)
    consts = [featk.astype(BF16), featq, cq, sdd, vfirst.astype(F32), vdiag, ones_rows]
    return pl.pallas_call(
        functools.partial(_attn_kernel, topk=topk),
        grid=(bsz, tp // TQ),
        in_specs=[qtile(A_WIDTH), qtile(A_WIDTH),
                  pl.BlockSpec((1, IDX_HEADS, TQ), lambda b, i: (b, 0, i)),
                  whole(tp, LANES), whole(tp, A_WIDTH), whole(A_WIDTH, tp)]
        + [pl.BlockSpec(c.shape, lambda b, i, nd=c.ndim: (0,) * nd, pipeline_mode=pl.Buffered(1))
           for c in consts],
        out_specs=qtile(A_WIDTH),
        out_shape=jax.ShapeDtypeStruct((bsz, tp, A_WIDTH), BF16),
        scratch_shapes=[pltpu.VMEM((tp + TQ, TQ), F32),
                        pltpu.VMEM((A_HEADS, TQ), F32),
                        pltpu.VMEM((A_HEADS, TQ), F32),
                        pltpu.VMEM((A_HEADS // 2, LANES, TQ), F32),
                        pltpu.VMEM((16, TQ), F32),
                        pltpu.VMEM((4, TQ, TQ), F32),
                        pltpu.VMEM((1, TQ), F32),
                        pltpu.VMEM((1, TQ), F32),
                        pltpu.VMEM((1, TQ), F32)],
        compiler_params=pltpu.CompilerParams(
            dimension_semantics=("arbitrary", "arbitrary"), vmem_limit_bytes=VMEM_LIMIT),
        name="attention",
    )(q, qi, wi, kx, k, vt, *consts)


def _split_dot(x, w):
    hi = x.astype(BF16)
    lo = (x - hi.astype(F32)).astype(BF16)
    return _dot(hi, w) + _dot(lo, w)


def _softplus(x):
    return jnp.maximum(x, 0.0) + jnp.log1p(jnp.exp(-jnp.abs(x)))


def _sigmoid(x):
    return 1.0 / (1.0 + jnp.exp(-x))


def _rwkv_kernel(pb_ref, w0_ref, wda_ref, a0_ref, wg_ref, kk_ref, ka_ref, rk_ref, gng_ref, gnb_ref,
                 ones_ref, bdm_ref, msl_ref, mil_ref, tri_ref,
                 o_ref, state_ref, lw_s, r_s, kn_s, ba_s, km_s, v_s, y_s, *, tr):
    j = pl.program_id(1)

    @pl.when(j == 0)
    def _():
        state_ref[...] = jnp.zeros_like(state_ref)

    pb = pb_ref[0]
    r = pb[:, 0:B_WIDTH]
    k = pb[:, B_WIDTH:2 * B_WIDTH]
    v = pb[:, 2 * B_WIDTH:3 * B_WIDTH]
    da = pb[:, 3 * B_WIDTH:3 * B_WIDTH + LANES]
    is_decay = lax.broadcasted_iota(jnp.int32, (tr, LANES), 1) < DECAY_LORA
    pre = _dot(jnp.where(is_decay, jnp.tanh(da), da).astype(BF16), wda_ref[...])
    w_log = -_softplus(-(w0_ref[...] + pre[:, :B_WIDTH])) - 0.5
    a = _sigmoid(a0_ref[...] + pre[:, B_WIDTH:])
    kk = k * kk_ref[...]
    ones_bd = ones_ref[...]
    norm = jnp.sqrt(_split_dot(kk * kk, ones_bd))
    kn = kk / jnp.maximum(norm, 1e-12)
    km = k * (1.0 + (a - 1.0) * ka_ref[...])
    lw_s[...] = -jnp.exp(w_log)
    r_s[...] = r
    kn_s[...] = kn
    ba_s[...] = kn * a
    km_s[...] = km
    v_s[...] = v

    bdm = bdm_ref[...]
    bdm_b = bdm.astype(BF16)
    msl = msl_ref[...]
    mil = mil_ref[...]
    eye_w = mil - msl
    tri = tri_ref[...]

    def bd4(x):
        return jnp.concatenate([x, x, x, x], axis=0) * bdm_b

    groups = range(B_WIDTH // GROUP)
    sls = [slice(g * GROUP, (g + 1) * GROUP) for g in groups]

    def free_stages(c):
        d = {}

        def prep():
            rows = pl.ds(pl.multiple_of(c * CHUNK, CHUNK), CHUNK)
            lw = lw_s[rows, :]
            cum = _split_dot_left(tri, lw)
            cum_end = cum[CHUNK - 1:CHUNK, :]
            e_neg = jnp.exp(-cum)
            e_end = jnp.exp(cum_end - cum)
            g_end = jnp.exp(cum_end)
            kn_c = kn_s[rows, :]
            ba_c = ba_s[rows, :]
            km_c = km_s[rows, :]
            vv = v_s[rows, :]
            a_t = -kn_c * jnp.exp(cum - lw)
            b_t = ba_c * e_neg
            k_t = km_c * e_neg
            r_t = r_s[rows, :] * jnp.exp(cum)
            b_h = ba_c * e_end
            k_h = km_c * e_end
            d["rows"] = rows
            d["a_b"] = [a_t[:, sl].astype(BF16) for sl in sls]
            d["r_b"] = [r_t[:, sl].astype(BF16) for sl in sls]
            d["b_bd"] = [bd4(b_t[:, sl].astype(BF16)) for sl in sls]
            d["k_bd"] = [bd4(k_t[:, sl].astype(BF16)) for sl in sls]
            d["v_b"] = [vv[:, sl].astype(BF16) for sl in sls]
            d["v_bd"] = [bd4(v) for v in d["v_b"]]
            d["bk"] = [jnp.concatenate([b_h[:, sl], k_h[:, sl]], axis=0).astype(BF16) for sl in sls]
            d["g_col"] = [jnp.transpose(jnp.broadcast_to(g_end[:, sl], (8, GROUP)))[:, 0:1] for sl in sls]

        def gram():
            ar = [jnp.concatenate([d["a_b"][g], d["r_b"][g]], axis=0) for g in groups]
            d["gb"] = [_dot_nt(ar[g], d["b_bd"][g]) for g in groups]
            d["gk"] = [_dot_nt(ar[g], d["k_bd"][g]) for g in groups]

        def masks():
            l_ab = [d["gb"][g][:CHUNK] * msl for g in groups]
            l_ak = [(d["gk"][g][:CHUNK] * msl).astype(BF16) for g in groups]
            g_rk = [(d["gk"][g][CHUNK:] * mil).astype(BF16) for g in groups]
            d["g_rb"] = [(d["gb"][g][CHUNK:] * mil).astype(BF16) for g in groups]
            d["lv"] = [_dot(l_ak[g], d["v_bd"][g]) for g in groups]
            d["yv"] = [_dot(g_rk[g], d["v_bd"][g]) for g in groups]
            d["x"] = [eye_w + l_ab[g] for g in groups]
            d["pw"] = l_ab

        def level():
            pwb = [d["pw"][g].astype(BF16) for g in groups]
            d["pw"] = [_dot(pwb[g], bd4(pwb[g])) for g in groups]
            d["x"] = [d["x"][g] + _dot(d["x"][g].astype(BF16), bd4(d["pw"][g].astype(BF16))) for g in groups]

        return d, [prep, gram, masks] + [level] * 5

    def state_stages(d, st):
        o = {}

        def enter():
            st_b = [st[g].astype(BF16) for g in groups]
            o["u_in"] = [_dot(d["a_b"][g], st_b[g]) + d["lv"][g] for g in groups]
            o["y0"] = [_dot(d["r_b"][g], st_b[g]) + d["yv"][g] for g in groups]

        def solve():
            o["u_b"] = [_dot(d["x"][g].astype(BF16), bd4(o["u_in"][g].astype(BF16))).astype(BF16)
                        for g in groups]

        def leave():
            o["st"] = []
            for g in groups:
                y_s[d["rows"], sls[g]] = o["y0"][g] + _dot(d["g_rb"][g], bd4(o["u_b"][g]))
                uv = jnp.concatenate([o["u_b"][g], d["v_b"][g]], axis=0)
                o["st"].append(st[g] * d["g_col"][g] + _dot_tn(d["bk"][g], uv) * bdm)

        return o, [enter, solve, leave]

    def chunk_pair(cp, _):
        d0, free0 = free_stages(2 * cp)
        for stage in free0:
            stage()
        d1, free1 = free_stages(2 * cp + 1)
        o0, state0 = state_stages(d0, [state_ref[g] for g in groups])
        slots = {0: state0[0], 2: state0[1], 4: state0[2]}
        for n, stage in enumerate(free1):
            if n in slots:
                slots[n]()
            stage()
        o1, state1 = state_stages(d1, o0["st"])
        for stage in state1:
            stage()
        for g in groups:
            state_ref[g] = o1["st"][g]
        return 0

    lax.fori_loop(0, tr // (2 * CHUNK), chunk_pair, 0)

    y = y_s[...]
    inv = 1.0 / B_HEAD_DIM
    mu = _split_dot(y, ones_bd) * inv
    yc = y - mu
    var = _split_dot(yc * yc, ones_bd) * inv
    yn = yc * lax.rsqrt(var + GN_EPS) * gng_ref[...] + gnb_ref[...]
    bonus = _split_dot(r_s[...] * km_s[...] * rk_ref[...], ones_bd) * v_s[...]
    gd = pb_ref[0][:, 3 * B_WIDTH + LANES:]
    gate = _dot(_sigmoid(gd).astype(BF16), wg_ref[...])
    o_ref[0] = ((yn + bonus) * gate).astype(BF16)


def _split_dot_left(w, x):
    hi = x.astype(BF16)
    lo = (x - hi.astype(F32)).astype(BF16)
    return _dot(w, hi) + _dot(w, lo)


def _rwkv(pbs, w0, wda, a0, wg, k_k, k_a, r_k, gn_g, gn_b, ones_bd, bdm, msl, mil, tri):
    bsz, tp, _ = pbs.shape
    tr = _row_tile(tp)
    consts = [w0, wda, a0, wg, k_k, k_a, r_k, gn_g, gn_b, ones_bd, bdm, msl, mil, tri]
    tok = lambda w: pl.BlockSpec((1, tr, w), lambda b, j: (b, j, 0))
    return pl.pallas_call(
        functools.partial(_rwkv_kernel, tr=tr),
        grid=(bsz, tp // tr),
        in_specs=[tok(B_COLS)] + [_const_spec(c.shape) for c in consts],
        out_specs=tok(B_WIDTH),
        out_shape=jax.ShapeDtypeStruct((bsz, tp, B_WIDTH), BF16),
        scratch_shapes=[pltpu.VMEM((B_WIDTH // GROUP, GROUP, GROUP), F32)]
        + [pltpu.VMEM((tr, B_WIDTH), F32)] * 7,
        compiler_params=pltpu.CompilerParams(
            dimension_semantics=("arbitrary", "arbitrary"), vmem_limit_bytes=VMEM_LIMIT),
        name="rwkv",
    )(pbs, *consts)


def _gelu_tanh(x):
    return 0.5 * x * (1.0 + jnp.tanh(0.7978845608028654 * (x + 0.044715 * (x * x * x))))


def _ffn_kernel(x_ref, ya_ref, yb_ref, lng_ref, lnb_ref, wout_ref, l1g_ref, l1b_ref, wup_ref,
                cw_ref, cb_ref, wdn_ref, l2g_ref, l2b_ref, o_ref, hist_ref, act_ref, *, tr):
    j = pl.program_id(1)

    @pl.when(j == 0)
    def _():
        hist_ref[...] = jnp.zeros_like(hist_ref)

    h0 = _layer_norm(x_ref[0], lng_ref[...], lnb_ref[...])
    mix = _dot(ya_ref[0], wout_ref[:A_WIDTH, :]) + _dot(yb_ref[0], wout_ref[A_WIDTH:, :])
    h1 = _layer_norm(ALPHA * h0 + mix, l1g_ref[...], l1b_ref[...])
    h1b = h1.astype(BF16)
    row = lax.broadcasted_iota(jnp.int32, (tr, 1), 0)
    valid = (j * tr + row) >= PAD

    def up(c):
        return (_dot(h1b, wup_ref[:, c * FF_CHUNK:(c + 1) * FF_CHUNK]),
                _dot(h1b, wup_ref[:, D_FF + c * FF_CHUNK:D_FF + (c + 1) * FF_CHUNK]))

    nxt = up(0)
    for c in range(N_FF_CHUNKS):
        cs = slice(c * FF_CHUNK, (c + 1) * FF_CHUNK)
        gate, val = nxt
        if c + 1 < N_FF_CHUNKS:
            nxt = up(c + 1)
        gate = jnp.where(valid, gate, 0.0)
        hist = hist_ref[c]
        g1 = jnp.where(row == 0, hist[7:8, :], pltpu.roll(gate, 1, 0))
        g2 = jnp.where(row == 0, hist[6:7, :], jnp.where(row == 1, hist[7:8, :], pltpu.roll(gate, 2, 0)))
        hist_ref[c] = gate[tr - 8:, :]
        conv = cw_ref[0:1, cs] * g2 + cw_ref[1:2, cs] * g1 + cw_ref[2:3, cs] * gate + cb_ref[:, cs]
        act_ref[:, cs] = (_gelu_tanh(conv) * val).astype(BF16)
    ffn = _dot(act_ref[...], wdn_ref[...])
    o_ref[0] = _layer_norm(ALPHA * h1 + ffn, l2g_ref[...], l2b_ref[...])


def _ffn(xcat, ya, yb, ln_g, ln_b, wout, l1g, l1b, wup, cw, cb, wdn, l2g, l2b):
    bsz, tp, _ = xcat.shape
    tr = _row_tile(tp)
    tok = lambda w: pl.BlockSpec((1, tr, w), lambda b, j: (b, j, 0))

    def res(c):
        nd = c.ndim
        return pl.BlockSpec(c.shape, lambda *_: (0,) * nd, pipeline_mode=pl.Buffered(1))

    return pl.pallas_call(
        functools.partial(_ffn_kernel, tr=tr),
        grid=(bsz, tp // tr),
        in_specs=[tok(D_MODEL), tok(A_WIDTH), tok(B_WIDTH), res(ln_g), res(ln_b), res(wout), res(l1g),
                  res(l1b), res(wup), res(cw), res(cb), res(wdn), res(l2g), res(l2b)],
        out_specs=tok(D_MODEL),
        out_shape=jax.ShapeDtypeStruct((bsz, tp, D_MODEL), F32),
        scratch_shapes=[pltpu.VMEM((N_FF_CHUNKS, 8, FF_CHUNK), F32), pltpu.VMEM((tr, D_FF), BF16)],
        compiler_params=pltpu.CompilerParams(
            dimension_semantics=("arbitrary", "arbitrary"), vmem_limit_bytes=VMEM_LIMIT),
        name="ffn",
    )(xcat, ya, yb, ln_g, ln_b, wout, l1g, l1b, wup, cw, cb, wdn, l2g, l2b)


def _row(v):
    return v.reshape(1, -1).astype(F32)


def kernel(x, meta, ln_in_g, ln_in_b, w_in, q_norm_g, w_uq, kv_norm_g, w_ukv, w_iq, idx_ln_g, idx_ln_b, mu_shift, w0, w_decay_up, a0, w_aaa_up, w_gate_up, k_k, k_a, r_k, gn_g, gn_b, w_out, ln1_g, ln1_b, w_ffn_up, conv_w, conv_b, w_ffn_down, ln2_g, ln2_b):
    bsz, seq, _ = x.shape
    assert seq % TQ == 0 and w_in.shape[0] == DEPTH
    t_real = N_META + seq
    topk = min(INDEX_TOPK, t_real // 4)

    xcat = jnp.concatenate([jnp.zeros((bsz, PAD, D_MODEL), x.dtype),
                            jnp.broadcast_to(meta[None].astype(x.dtype), (bsz, N_META, D_MODEL)), x], axis=1)

    wi = w_in[0]
    kidx_cols = wi[:, Q_RANK + KV_RANK:Q_RANK + KV_RANK + IDX_DIM]
    win = jnp.concatenate([
        wi[:, :Q_RANK + KV_RANK], kidx_cols, kidx_cols, wi[:, Q_RANK + KV_RANK + IDX_DIM:A_COLS],
        jnp.zeros((D_MODEL, A_PAD - 512 - IDX_HEADS), F32), wi[:, A_COLS:]], axis=1).astype(BF16)
    wukv = w_ukv[0].reshape(KV_RANK, A_HEADS, 2, A_HEAD_DIM)
    wuk = wukv[:, :, 0].reshape(KV_RANK, A_WIDTH).astype(BF16)
    wuvt = wukv[:, :, 1].reshape(KV_RANK, A_WIDTH).T.astype(BF16)
    ig2 = _row(jnp.concatenate([idx_ln_g[0], idx_ln_g[0]]))
    ib2 = _row(jnp.concatenate([idx_ln_b[0], idx_ln_b[0]]))

    q, qi, widx, k, vt, kx, pbs = _inproj(
        xcat, _row(ln_in_g), _row(ln_in_b), win, _row(q_norm_g[0]), w_uq[0].astype(BF16),
        w_iq[0].astype(BF16), _row(kv_norm_g[0]), wuk, wuvt, ig2, ib2, _row(mu_shift[0]))

    ya = _attention(q, qi, widx, kx, k, vt, topk)

    zl = jnp.zeros((DECAY_LORA, B_WIDTH), F32)
    wda = jnp.concatenate([jnp.concatenate([w_decay_up[0], zl], axis=1),
                           jnp.concatenate([zl, w_aaa_up[0]], axis=1)], axis=0).astype(BF16)
    ci = jnp.arange(B_WIDTH)
    ones_bd = (ci[:, None] // B_HEAD_DIM == ci[None, :] // B_HEAD_DIM).astype(BF16)
    gi = jnp.arange(GROUP)
    bdm = (gi[:, None] // B_HEAD_DIM == gi[None, :] // B_HEAD_DIM).astype(F32)
    ti = jnp.arange(CHUNK)
    msl = (ti[:, None] > (gi[None, :] % CHUNK)).astype(F32)
    mil = (ti[:, None] >= (gi[None, :] % CHUNK)).astype(F32)
    tri = (ti[:, None] >= ti[None, :]).astype(BF16)
    yb = _rwkv(pbs, _row(w0[0]), wda, _row(a0[0]), w_gate_up[0].astype(BF16), _row(k_k[0]), _row(k_a[0]),
               _row(r_k[0]), _row(gn_g[0]), _row(gn_b[0]), ones_bd, bdm, msl, mil, tri)

    out = _ffn(xcat, ya, yb, _row(ln_in_g), _row(ln_in_b), w_out[0].astype(BF16), _row(ln1_g[0]),
               _row(ln1_b[0]), w_ffn_up[0].astype(BF16), conv_w[0].astype(F32), _row(conv_b[0]),
               w_ffn_down[0].astype(BF16), _row(ln2_g[0]), _row(ln2_b[0]))
    return out[:, OFF:]
```

```python
import functools

import jax
import jax.numpy as jnp
from jax import lax
from jax.experimental import pallas as pl
from jax.experimental.pallas import tpu as pltpu

F32 = jnp.float32
BF16 = jnp.bfloat16

D_MODEL = 1024
N_META = 16
CHUNK = 64
A_HEADS = 8
A_HEAD_DIM = 64
A_WIDTH = A_HEADS * A_HEAD_DIM
Q_RANK = 256
KV_RANK = 128
IDX_HEADS = 8
IDX_DIM = 64
INDEX_TOPK = 256
B_HEADS = 8
B_HEAD_DIM = 64
B_WIDTH = B_HEADS * B_HEAD_DIM
DECAY_LORA = 64
AAA_LORA = 64
GATE_LORA = 128
GN_EPS = 64e-5
D_FF = 2816
LN_EPS = 1e-5
DEPTH = 1
ALPHA = (2 * DEPTH) ** 0.25
A_COLS = Q_RANK + KV_RANK + IDX_DIM + IDX_HEADS
B_COLS = 3 * B_WIDTH + DECAY_LORA + AAA_LORA + GATE_LORA
A_PAD = 640
IN_PAD = A_PAD + B_COLS

LANES = 128
TQ = 256
OFF = TQ
PAD = OFF - N_META
FF_CHUNK = 256
N_FF_CHUNKS = D_FF // FF_CHUNK
GROUP = 256
NEG = -1e30
LOG2E = 1.4426950408889634
VMEM_LIMIT = 56 * 1024 * 1024


def _row_tile(tp):
    for cand in (768, 512, 256):
        if tp % cand == 0:
            return cand
    raise ValueError(tp)


def _layer_norm(x, g, b):
    mu = jnp.mean(x, -1, keepdims=True)
    xc = x - mu
    var = jnp.mean(xc * xc, -1, keepdims=True)
    return xc * lax.rsqrt(var + LN_EPS) * g + b


def _dot(a, b):
    return jnp.dot(a, b, preferred_element_type=F32)


def _dot_nt(a, b):
    return lax.dot_general(a, b, (((1,), (1,)), ((), ())), preferred_element_type=F32)


def _dot_tn(a, b):
    return lax.dot_general(a, b, (((0,), (0,)), ((), ())), preferred_element_type=F32)


def _const_spec(shape):
    nd = len(shape)
    return pl.BlockSpec(shape, lambda *_: (0,) * nd)


def _inproj_kernel(x_ref, lng_ref, lnb_ref, win_ref, qg_ref, wuq_ref, wiq_ref, kvg_ref, wuk_ref,
                   wuvt_ref, ig_ref, ib_ref, mu_ref,
                   q_ref, qi_ref, wi_ref, k_ref, vt_ref, kx_ref, pb_ref, carry_ref, *, tr):
    j = pl.program_id(1)

    @pl.when(j == 0)
    def _():
        carry_ref[...] = jnp.zeros_like(carry_ref)

    h = _layer_norm(x_ref[0], lng_ref[...], lnb_ref[...])
    pos = j * tr + lax.broadcasted_iota(jnp.int32, (tr, 1), 0)
    valid = pos >= PAD
    h = jnp.where(valid, h, 0.0)
    p = _dot(h.astype(BF16), win_ref[...])

    cq = p[:, :Q_RANK]
    cq = cq * lax.rsqrt(jnp.mean(cq * cq, -1, keepdims=True) + 1e-6) * qg_ref[...]
    cqb = cq.astype(BF16)
    q_ref[0] = (_dot(cqb, wuq_ref[...]) * (A_HEAD_DIM ** -0.5 * LOG2E)).astype(BF16)
    qi_ref[0] = _dot(cqb, wiq_ref[...]).astype(BF16)
    ckv = p[:, Q_RANK:Q_RANK + KV_RANK]
    ckv = ckv * lax.rsqrt(jnp.mean(ckv * ckv, -1, keepdims=True) + 1e-6) * kvg_ref[...]
    ckvb = ckv.astype(BF16)
    k_ref[0] = _dot(ckvb, wuk_ref[...]).astype(BF16)
    vt_ref[0] = _dot_nt(wuvt_ref[...], ckvb).astype(BF16)
    kx_ref[0] = _layer_norm(p[:, 384:512], ig_ref[...], ib_ref[...]).astype(BF16)
    wi_ref[0] = p[:, 512:A_PAD].T[:IDX_HEADS, :] * (IDX_HEADS ** -0.5 * IDX_DIM ** -0.5)

    pb = p[:, A_PAD:]
    prev = carry_ref[7:8, :]
    row = lax.broadcasted_iota(jnp.int32, (tr, 1), 0)
    shifted = jnp.where(row == 0, prev, pltpu.roll(pb, 1, 0))
    carry_ref[...] = pb[tr - 8:, :]
    pbs = pb + (shifted - pb) * mu_ref[...]
    pb_ref[0] = jnp.where(valid, pbs, 0.0)


def _inproj(xcat, ln_g, ln_b, win, qg, wuq, wiq, kvg, wuk, wuvt, ig2, ib2, mu):
    bsz, tp, _ = xcat.shape
    tr = _row_tile(tp)
    tok = lambda w: pl.BlockSpec((1, tr, w), lambda b, j: (b, j, 0))
    chan = lambda c: pl.BlockSpec((1, c, tr), lambda b, j: (b, 0, j))
    consts = [ln_g, ln_b, win, qg, wuq, wiq, kvg, wuk, wuvt, ig2, ib2, mu]
    return pl.pallas_call(
        functools.partial(_inproj_kernel, tr=tr),
        grid=(bsz, tp // tr),
        in_specs=[tok(D_MODEL)] + [_const_spec(c.shape) for c in consts],
        out_specs=[tok(A_WIDTH), tok(A_WIDTH), chan(IDX_HEADS), tok(A_WIDTH), chan(A_WIDTH),
                   tok(LANES), tok(B_COLS)],
        out_shape=[jax.ShapeDtypeStruct((bsz, tp, A_WIDTH), BF16),
                   jax.ShapeDtypeStruct((bsz, tp, A_WIDTH), BF16),
                   jax.ShapeDtypeStruct((bsz, IDX_HEADS, tp), F32),
                   jax.ShapeDtypeStruct((bsz, tp, A_WIDTH), BF16),
                   jax.ShapeDtypeStruct((bsz, A_WIDTH, tp), BF16),
                   jax.ShapeDtypeStruct((bsz, tp, LANES), BF16),
                   jax.ShapeDtypeStruct((bsz, tp, B_COLS), F32)],
        scratch_shapes=[pltpu.VMEM((8, B_COLS), F32)],
        compiler_params=pltpu.CompilerParams(
            dimension_semantics=("arbitrary", "arbitrary"), vmem_limit_bytes=VMEM_LIMIT),
        name="inproj",
    )(xcat, *consts)


def _chunk_of(pos):
    return jnp.maximum((pos - OFF + CHUNK) >> 6, 0)


def _pair_blockdiag(slab, lo_half):
    zero = jnp.zeros_like(slab)
    return jnp.concatenate([jnp.where(lo_half, slab, zero), jnp.where(lo_half, zero, slab)], axis=0)


def _attn_kernel(q_ref, qi_ref, w_ref, kx_ref, k_ref, vt_ref, featk_ref, featq_ref, cq_ref, sdd_ref,
                 vfirst_ref, vdiag_ref, ones_ref,
                 o_ref, s_ref, m_ref, l_ref, acc_ref, st_ref, bias_ref, tcar_ref, rmax_ref, rmin_ref,
                 *, topk):
    i = pl.program_id(1)
    nkt = i + 1
    lo_lane = lax.broadcasted_iota(jnp.int32, (TQ, LANES), 1) < A_HEAD_DIM
    lo_row = lax.broadcasted_iota(jnp.int32, (LANES, TQ), 0) < A_HEAD_DIM
    qp = i * TQ + lax.broadcasted_iota(jnp.int32, (1, TQ), 1)
    qc = _chunk_of(qp)
    kf = float(topk)

    qi = qi_ref[0]
    w = w_ref[0]

    def score_tiles(kts, carry):
        rmax, rmin = carry
        bases = [pl.multiple_of(kt * TQ, TQ) for kt in kts]
        z_idx = [_pair_blockdiag(kx_ref[0, pl.ds(base, TQ), :], lo_lane) for base in bases]
        zz = [[_dot_nt(z, qi[:, jp * LANES:(jp + 1) * LANES]) for jp in range(IDX_HEADS // 2)]
              for z in z_idx]
        for base, zt in zip(bases, zz):
            acc = jnp.zeros((TQ, TQ), F32)
            for jp, z in enumerate(zt):
                r = jnp.maximum(z, 0.0)
                acc = acc + r[:TQ] * w[2 * jp:2 * jp + 1] + r[TQ:] * w[2 * jp + 1:2 * jp + 2]
            s_ref[pl.ds(base, TQ), :] = acc
            rmax = jnp.maximum(rmax, jnp.max(acc, axis=0, keepdims=True))
            rmin = jnp.minimum(rmin, jnp.min(acc, axis=0, keepdims=True))
        return rmax, rmin

    ext = score_tiles([0], (jnp.full((1, TQ), -jnp.inf, F32), jnp.full((1, TQ), jnp.inf, F32)))
    ext = lax.fori_loop(0, i // 2, lambda g, c: score_tiles([2 * g + 1, 2 * g + 2], c), ext)
    rmax_ref[...] = ext[0]
    rmin_ref[...] = ext[1]

    @pl.when((i & 1) == 1)
    def _():
        last = score_tiles([i], (rmax_ref[...], rmin_ref[...]))
        rmax_ref[...] = last[0]
        rmin_ref[...] = last[1]

    rmax = rmax_ref[...]
    rmin = rmin_ref[...]
    s_ref[0:TQ, :] = s_ref[0:TQ, :] + vfirst_ref[...]

    @pl.when(i > 0)
    def _():
        dbase = pl.multiple_of(i * TQ, TQ)
        s_ref[pl.ds(dbase, TQ), :] = s_ref[pl.ds(dbase, TQ), :] + vdiag_ref[...]

    s_ref[pl.ds(pl.multiple_of(nkt * TQ, TQ), TQ), :] = jnp.full((TQ, TQ), -jnp.inf, F32)

    def count_above(thr):
        def cnt_body(g, part):
            for half in range(2):
                base = pl.multiple_of(g * (2 * TQ) + half * TQ, TQ)
                for r0 in range(0, TQ, 64):
                    s = s_ref[pl.ds(base + r0, 64), :]
                    hit = jnp.where(s > thr, 1.0, 0.0)
                    part = part + jnp.sum(hit.reshape(8, 8, TQ), axis=0)
            return part

        part = lax.fori_loop(0, (nkt + 1) // 2, cnt_body, jnp.zeros((8, TQ), F32))
        return jnp.sum(part, axis=0, keepdims=True)

    nvis = (N_META + CHUNK * qc).astype(F32)
    zero = jnp.zeros((1, TQ), F32)
    def zero_counts(g, parts):
        gt, ge = parts
        for half in range(2):
            base = pl.multiple_of(g * (2 * TQ) + half * TQ, TQ)
            for r0 in range(0, TQ, 64):
                s = s_ref[pl.ds(base + r0, 64), :]
                gt = gt + jnp.sum(jnp.where(s > 0.0, 1.0, 0.0).reshape(8, 8, TQ), axis=0)
                ge = ge + jnp.sum(jnp.where(s >= 0.0, 1.0, 0.0).reshape(8, 8, TQ), axis=0)
        return gt, ge

    gt0, ge0 = lax.fori_loop(0, (nkt + 1) // 2, zero_counts,
                             (jnp.zeros((8, TQ), F32), jnp.zeros((8, TQ), F32)))
    c_gt0 = jnp.sum(gt0, axis=0, keepdims=True)
    c_ge0 = jnp.sum(ge0, axis=0, keepdims=True)
    all_visible = nvis <= kf
    at_zero = jnp.logical_and(c_gt0 <= kf, c_ge0 >= kf)
    positive = c_gt0 > kf
    below_min = rmin - jnp.maximum(jnp.abs(rmin) * (2.0 ** -20), 1e-30)
    st_ref[0:1, :] = jnp.where(positive, zero, below_min)
    st_ref[1:2, :] = jnp.where(positive, rmax, zero)
    st_ref[3:4, :] = jnp.where(positive, zero, c_gt0)
    st_ref[2:3, :] = jnp.where(positive, c_gt0, nvis) - kf
    st_ref[7:8, :] = kf - jnp.where(positive, zero, c_gt0)
    st_ref[8:9, :] = zero
    st_ref[4:5, :] = jnp.where(all_visible, -jnp.inf, zero)
    st_ref[5:6, :] = jnp.where(jnp.logical_and(at_zero, jnp.logical_not(all_visible)), kf - c_gt0, zero)
    st_ref[6:7, :] = jnp.where(jnp.logical_or(all_visible, at_zero), 1.0, 0.0)

    def search_cond(c):
        it, active = c
        return jnp.logical_and(active > 0.0, it < 400)

    def search_body(c):
        it, _ = c
        lo = st_ref[0:1, :]
        hi = st_ref[1:2, :]
        chi = st_ref[3:4, :]
        f_lo = st_ref[2:3, :]
        g_hi = st_ref[7:8, :]
        side = st_ref[8:9, :]
        thr = st_ref[4:5, :]
        need = st_ref[5:6, :]
        done = st_ref[6:7, :]
        sign = jnp.where(hi > 0.0, 1.0, -1.0)
        small = jnp.minimum(jnp.abs(lo), jnp.abs(hi))
        large = jnp.maximum(jnp.abs(lo), jnp.abs(hi))
        geo = sign * jnp.where(small == 0.0, large * (2.0 ** -12), jnp.sqrt(small * large))
        halve = jnp.where(large > 4.0 * small, geo, lo + (hi - lo) * 0.5)
        lin = lo + (hi - lo) * jnp.clip(f_lo / (f_lo + g_hi), 0.002, 0.998)
        use_lin = jnp.logical_and(jnp.logical_and(lin > lo, lin < hi), (it & 3) != 3)
        mid = jnp.where(use_lin, lin, halve)
        inside = jnp.logical_and(mid > lo, mid < hi)
        cnt = count_above(mid)
        nd = done < 0.5
        nd_in = jnp.logical_and(nd, inside)
        exact = jnp.logical_and(nd_in, cnt == kf)
        coll = jnp.logical_and(nd, jnp.logical_not(inside))
        go_lo = jnp.logical_and(nd_in, cnt > kf)
        go_hi = jnp.logical_and(nd_in, cnt < kf)
        st_ref[4:5, :] = jnp.where(exact, mid, jnp.where(coll, hi, thr))
        st_ref[5:6, :] = jnp.where(coll, kf - chi, need)
        done = jnp.where(jnp.logical_or(exact, coll), 1.0, done)
        st_ref[6:7, :] = done
        st_ref[0:1, :] = jnp.where(go_lo, mid, lo)
        st_ref[1:2, :] = jnp.where(go_hi, mid, hi)
        st_ref[3:4, :] = jnp.where(go_hi, cnt, chi)
        st_ref[2:3, :] = jnp.where(go_lo, cnt - kf,
                                   jnp.where(jnp.logical_and(go_hi, side < 0.0), f_lo * 0.5, f_lo))
        st_ref[7:8, :] = jnp.where(go_hi, kf - cnt,
                                   jnp.where(jnp.logical_and(go_lo, side > 0.0), g_hi * 0.5, g_hi))
        st_ref[8:9, :] = jnp.where(go_lo, 1.0, jnp.where(go_hi, -1.0, side))
        return it + 1, jnp.max(1.0 - done)

    lax.while_loop(search_cond, search_body, (jnp.int32(0), jnp.max(1.0 - st_ref[6:7, :])))
    unfinished = st_ref[6:7, :] < 0.5
    thr = jnp.where(unfinished, st_ref[1:2, :], st_ref[4:5, :])
    need = jnp.where(unfinished, kf - st_ref[3:4, :], st_ref[5:6, :])
    tie_any = jnp.max(need) > 0.0

    m_ref[...] = jnp.full(m_ref.shape, NEG, F32)
    l_ref[...] = jnp.zeros_like(l_ref)
    acc_ref[...] = jnp.zeros_like(acc_ref)
    tcar_ref[...] = jnp.zeros_like(tcar_ref)
    q = q_ref[0]

    def select_bias(kt, slot):
        base = pl.multiple_of(kt * TQ, TQ)
        s = s_ref[pl.ds(base, TQ), :]
        bias_ref[slot] = jnp.where(s > thr, 0.0, NEG)

        @pl.when(tie_any)
        def _():
            tie = jnp.logical_and(s == thr, need > 0.0)
            tie_b = jnp.where(tie, 1.0, 0.0).astype(BF16)
            lower = (lax.broadcasted_iota(jnp.int32, (TQ, TQ), 1)
                     <= lax.broadcasted_iota(jnp.int32, (TQ, TQ), 0))
            prefix = _dot(jnp.where(lower, 1.0, 0.0).astype(BF16), tie_b)
            rank = tcar_ref[...] + prefix
            take = jnp.logical_and(tie, rank <= need)
            bias_ref[slot] = jnp.where(jnp.logical_or(s > thr, take), 0.0, NEG)
            tcar_ref[...] = tcar_ref[...] + prefix[TQ - 1:TQ, :]

    def att_tiles(kts, diag):
        for slot, kt in enumerate(kts):
            select_bias(kt, slot)
        bases = [pl.multiple_of(kt * TQ, TQ) for kt in kts]
        kks = [k_ref[0, pl.ds(base, TQ), :] for base in bases]
        gaps = [((i - kt) * TQ).astype(F32) for kt in kts]
        lgs = []
        for jp in range(A_HEADS // 2):
            cols = slice(jp * LANES, (jp + 1) * LANES)
            if diag:
                zk = _pair_blockdiag(kks[0][:, cols], lo_lane)
                lgs.append(_dot_nt(zk, q[:, cols]))
            else:
                zk = jnp.concatenate(
                    [jnp.concatenate([_pair_blockdiag(kk[:, cols], lo_lane), featk_ref[...]], axis=1)
                     for kk in kks], axis=0)
                lgs.append(_dot_nt(zk, jnp.concatenate([q[:, cols], featq_ref[jp]], axis=1)))
        for jp in range(A_HEADS // 2):
            cols = slice(jp * LANES, (jp + 1) * LANES)
            lg = lgs[jp]
            ps, alphas = [[None, None] for _ in kts], []
            for hh in range(2):
                h = 2 * jp + hh
                lghs, shifts = [], []
                for t in range(len(kts)):
                    r0 = (2 * t + hh) * TQ
                    if diag:
                        lghs.append(lg[r0:r0 + TQ] - sdd_ref[h] + bias_ref[t])
                        shifts.append(0.0)
                    else:
                        lghs.append(lg[r0:r0 + TQ] + bias_ref[t])
                        shifts.append((LOG2E * 2.0 ** -(h + 1)) * gaps[t] + cq_ref[h:h + 1, :])
                m_old = m_ref[h:h + 1, :]
                m_new = m_old
                for lgh, shift in zip(lghs, shifts):
                    m_new = jnp.maximum(m_new, jnp.max(lgh, axis=0, keepdims=True) - shift)
                alpha = jnp.exp2(m_old - m_new)
                for t, (lgh, shift) in enumerate(zip(lghs, shifts)):
                    ps[t][hh] = jnp.exp2((lgh - (m_new + shift)).astype(BF16))
                m_ref[h:h + 1, :] = m_new
                alphas.append(alpha)
            zvts = []
            for base in bases:
                vt = vt_ref[0, cols, pl.ds(base, TQ)]
                zero = jnp.zeros_like(vt)
                zvts += [jnp.where(lo_row, vt, zero), jnp.where(lo_row, zero, vt)]
            lhs = jnp.concatenate([jnp.concatenate(zvts, axis=1),
                                   jnp.concatenate([ones_ref[...]] * len(kts), axis=1)], axis=0)
            pv = _dot(lhs, jnp.concatenate([p for pt in ps for p in pt], axis=0))
            acc_ref[jp] = acc_ref[jp] * jnp.where(lo_row, alphas[0], alphas[1]) + pv[:LANES]
            for hh in range(2):
                h = 2 * jp + hh
                l_ref[h:h + 1, :] = alphas[hh] * l_ref[h:h + 1, :] + pv[LANES + hh:LANES + hh + 1]

    def att_body(g, _):
        att_tiles([4 * g + t for t in range(4)], False)
        return 0

    lax.fori_loop(0, i // 4, att_body, 0)
    done4 = (i // 4) * 4

    @pl.when((i & 2) == 2)
    def _():
        att_tiles([done4, done4 + 1], False)

    @pl.when((i & 1) == 1)
    def _():
        att_tiles([i - 1], False)

    att_tiles([i], True)
    for jp in range(A_HEADS // 2):
        denom = jnp.where(lo_row, l_ref[2 * jp:2 * jp + 1, :], l_ref[2 * jp + 1:2 * jp + 2, :])
        o_ref[0, :, jp * LANES:(jp + 1) * LANES] = (acc_ref[jp] / denom).T.astype(BF16)


def _attention(q, qi, wi, kx, k, vt, topk):
    bsz, tp, _ = q.shape
    qtile = lambda w: pl.BlockSpec((1, TQ, w), lambda b, i: (b, i, 0))
    whole = lambda r, c: pl.BlockSpec((1, r, c), lambda b, i: (b, 0, 0), pipeline_mode=pl.Buffered(1))
    kq = jnp.arange(TQ)
    slopes = (LOG2E * 2.0 ** -(jnp.arange(A_HEADS) + 1.0)).astype(F32)
    s0 = slopes.astype(BF16)
    s1 = (slopes - s0.astype(F32)).astype(BF16)
    s2 = (slopes - s0.astype(F32) - s1.astype(F32)).astype(BF16)
    parts = jnp.stack([s0, s1, s2], axis=1)
    col = jnp.arange(LANES)
    kloc = kq.astype(BF16)[:, None]
    featk = jnp.concatenate([jnp.where(col[None, :] < 3, kloc, 0),
                             jnp.where((col[None, :] >= 3) & (col[None, :] < 6), kloc, 0)], axis=0)
    featq_rows = jnp.zeros((A_HEADS // 2, LANES), BF16)
    featq_rows = featq_rows.at[:, 0:3].set(parts[0::2]).at[:, 3:6].set(parts[1::2])
    featq = jnp.broadcast_to(featq_rows[:, None, :], (A_HEADS // 2, TQ, LANES))
    cq = slopes[:, None] * kq[None, :].astype(F32)
    rel = (kq[None, :] - kq[:, None]).astype(F32)
    sdd = slopes[:, None, None] * jnp.abs(rel)[None]
    vfirst = jnp.where(kq[:, None] >= PAD, 0.0, -jnp.inf) * jnp.ones((1, TQ), F32)
    vdiag = jnp.where((kq[:, None] // CHUNK) <= (kq[None, :] // CHUNK), 0.0, -jnp.inf).astype(F32)
    kcol = jnp.arange(2 * TQ)[None, :]
    orow = jnp.arange(16)[:, None]
    ones_rows = jnp.where(((orow == 0) & (kcol < TQ)) | ((orow == 1) & (kcol >= TQ)), 1.0, 0.0).astype(BF16)
    consts = [featk.astype(BF16), featq, cq, sdd, vfirst.astype(F32), vdiag, ones_rows]
    return pl.pallas_call(
        functools.partial(_attn_kernel, topk=topk),
        grid=(bsz, tp // TQ),
        in_specs=[qtile(A_WIDTH), qtile(A_WIDTH),
                  pl.BlockSpec((1, IDX_HEADS, TQ), lambda b, i: (b, 0, i)),
                  whole(tp, LANES), whole(tp, A_WIDTH), whole(A_WIDTH, tp)]
        + [pl.BlockSpec(c.shape, lambda b, i, nd=c.ndim: (0,) * nd, pipeline_mode=pl.Buffered(1))
           for c in consts],
        out_specs=qtile(A_WIDTH),
        out_shape=jax.ShapeDtypeStruct((bsz, tp, A_WIDTH), BF16),
        scratch_shapes=[pltpu.VMEM((tp + TQ, TQ), F32),
                        pltpu.VMEM((A_HEADS, TQ), F32),
                        pltpu.VMEM((A_HEADS, TQ), F32),
                        pltpu.VMEM((A_HEADS // 2, LANES, TQ), F32),
                        pltpu.VMEM((16, TQ), F32),
                        pltpu.VMEM((4, TQ, TQ), F32),
                        pltpu.VMEM((1, TQ), F32),
                        pltpu.VMEM((1, TQ), F32),
                        pltpu.VMEM((1, TQ), F32)],
        compiler_params=pltpu.CompilerParams(
            dimension_semantics=("arbitrary", "arbitrary"), vmem_limit_bytes=VMEM_LIMIT),
        name="attention",
    )(q, qi, wi, kx, k, vt, *consts)


def _split_dot(x, w):
    hi = x.astype(BF16)
    lo = (x - hi.astype(F32)).astype(BF16)
    return _dot(hi, w) + _dot(lo, w)


def _softplus(x):
    return jnp.maximum(x, 0.0) + jnp.log1p(jnp.exp(-jnp.abs(x)))


def _sigmoid(x):
    return 1.0 / (1.0 + jnp.exp(-x))


def _rwkv_kernel(pb_ref, w0_ref, wda_ref, a0_ref, wg_ref, kk_ref, ka_ref, rk_ref, gng_ref, gnb_ref,
                 ones_ref, bdm_ref, msl_ref, mil_ref, tri_ref,
                 o_ref, state_ref, lw_s, r_s, kn_s, ba_s, km_s, v_s, y_s, *, tr):
    j = pl.program_id(1)

    @pl.when(j == 0)
    def _():
        state_ref[...] = jnp.zeros_like(state_ref)

    pb = pb_ref[0]
    r = pb[:, 0:B_WIDTH]
    k = pb[:, B_WIDTH:2 * B_WIDTH]
    v = pb[:, 2 * B_WIDTH:3 * B_WIDTH]
    da = pb[:, 3 * B_WIDTH:3 * B_WIDTH + LANES]
    is_decay = lax.broadcasted_iota(jnp.int32, (tr, LANES), 1) < DECAY_LORA
    pre = _dot(jnp.where(is_decay, jnp.tanh(da), da).astype(BF16), wda_ref[...])
    w_log = -_softplus(-(w0_ref[...] + pre[:, :B_WIDTH])) - 0.5
    a = _sigmoid(a0_ref[...] + pre[:, B_WIDTH:])
    kk = k * kk_ref[...]
    ones_bd = ones_ref[...]
    norm = jnp.sqrt(_split_dot(kk * kk, ones_bd))
    kn = kk / jnp.maximum(norm, 1e-12)
    km = k * (1.0 + (a - 1.0) * ka_ref[...])
    lw_s[...] = -jnp.exp(w_log)
    r_s[...] = r
    kn_s[...] = kn
    ba_s[...] = kn * a
    km_s[...] = km
    v_s[...] = v

    bdm = bdm_ref[...]
    bdm_b = bdm.astype(BF16)
    msl = msl_ref[...]
    mil = mil_ref[...]
    eye_w = mil - msl
    tri = tri_ref[...]

    def bd4(x):
        return jnp.concatenate([x, x, x, x], axis=0) * bdm_b

    groups = range(B_WIDTH // GROUP)
    sls = [slice(g * GROUP, (g + 1) * GROUP) for g in groups]

    def free_stages(c):
        d = {}

        def prep():
            rows = pl.ds(pl.multiple_of(c * CHUNK, CHUNK), CHUNK)
            lw = lw_s[rows, :]
            cum = _split_dot_left(tri, lw)
            cum_end = cum[CHUNK - 1:CHUNK, :]
            e_neg = jnp.exp(-cum)
            e_end = jnp.exp(cum_end - cum)
            g_end = jnp.exp(cum_end)
            kn_c = kn_s[rows, :]
            ba_c = ba_s[rows, :]
            km_c = km_s[rows, :]
            vv = v_s[rows, :]
            a_t = -kn_c * jnp.exp(cum - lw)
            b_t = ba_c * e_neg
            k_t = km_c * e_neg
            r_t = r_s[rows, :] * jnp.exp(cum)
            b_h = ba_c * e_end
            k_h = km_c * e_end
            d["rows"] = rows
            d["a_b"] = [a_t[:, sl].astype(BF16) for sl in sls]
            d["r_b"] = [r_t[:, sl].astype(BF16) for sl in sls]
            d["b_bd"] = [bd4(b_t[:, sl].astype(BF16)) for sl in sls]
            d["k_bd"] = [bd4(k_t[:, sl].astype(BF16)) for sl in sls]
            d["v_b"] = [vv[:, sl].astype(BF16) for sl in sls]
            d["v_bd"] = [bd4(v) for v in d["v_b"]]
            d["bk"] = [jnp.concatenate([b_h[:, sl], k_h[:, sl]], axis=0).astype(BF16) for sl in sls]
            d["g_col"] = [jnp.transpose(jnp.broadcast_to(g_end[:, sl], (8, GROUP)))[:, 0:1] for sl in sls]

        def gram():
            ar = [jnp.concatenate([d["a_b"][g], d["r_b"][g]], axis=0) for g in groups]
            d["gb"] = [_dot_nt(ar[g], d["b_bd"][g]) for g in groups]
            d["gk"] = [_dot_nt(ar[g], d["k_bd"][g]) for g in groups]

        def masks():
            l_ab = [d["gb"][g][:CHUNK] * msl for g in groups]
            l_ak = [(d["gk"][g][:CHUNK] * msl).astype(BF16) for g in groups]
            g_rk = [(d["gk"][g][CHUNK:] * mil).astype(BF16) for g in groups]
            d["g_rb"] = [(d["gb"][g][CHUNK:] * mil).astype(BF16) for g in groups]
            d["lv"] = [_dot(l_ak[g], d["v_bd"][g]) for g in groups]
            d["yv"] = [_dot(g_rk[g], d["v_bd"][g]) for g in groups]
            d["x"] = [eye_w + l_ab[g] for g in groups]
            d["pw"] = l_ab

        def level():
            pwb = [d["pw"][g].astype(BF16) for g in groups]
            d["pw"] = [_dot(pwb[g], bd4(pwb[g])) for g in groups]
            d["x"] = [d["x"][g] + _dot(d["x"][g].astype(BF16), bd4(d["pw"][g].astype(BF16))) for g in groups]

        return d, [prep, gram, masks] + [level] * 5

    def state_stages(d, st):
        o = {}

        def enter():
            st_b = [st[g].astype(BF16) for g in groups]
            o["u_in"] = [_dot(d["a_b"][g], st_b[g]) + d["lv"][g] for g in groups]
            o["y0"] = [_dot(d["r_b"][g], st_b[g]) + d["yv"][g] for g in groups]

        def solve():
            o["u_b"] = [_dot(d["x"][g].astype(BF16), bd4(o["u_in"][g].astype(BF16))).astype(BF16)
                        for g in groups]

        def leave():
            o["st"] = []
            for g in groups:
                y_s[d["rows"], sls[g]] = o["y0"][g] + _dot(d["g_rb"][g], bd4(o["u_b"][g]))
                uv = jnp.concatenate([o["u_b"][g], d["v_b"][g]], axis=0)
                o["st"].append(st[g] * d["g_col"][g] + _dot_tn(d["bk"][g], uv) * bdm)

        return o, [enter, solve, leave]

    def chunk_pair(cp, _):
        d0, free0 = free_stages(2 * cp)
        for stage in free0:
            stage()
        d1, free1 = free_stages(2 * cp + 1)
        o0, state0 = state_stages(d0, [state_ref[g] for g in groups])
        slots = {0: state0[0], 2: state0[1], 4: state0[2]}
        for n, stage in enumerate(free1):
            if n in slots:
                slots[n]()
            stage()
        o1, state1 = state_stages(d1, o0["st"])
        for stage in state1:
            stage()
        for g in groups:
            state_ref[g] = o1["st"][g]
        return 0

    lax.fori_loop(0, tr // (2 * CHUNK), chunk_pair, 0)

    y = y_s[...]
    inv = 1.0 / B_HEAD_DIM
    mu = _split_dot(y, ones_bd) * inv
    yc = y - mu
    var = _split_dot(yc * yc, ones_bd) * inv
    yn = yc * lax.rsqrt(var + GN_EPS) * gng_ref[...] + gnb_ref[...]
    bonus = _split_dot(r_s[...] * km_s[...] * rk_ref[...], ones_bd) * v_s[...]
    gd = pb_ref[0][:, 3 * B_WIDTH + LANES:]
    gate = _dot(_sigmoid(gd).astype(BF16), wg_ref[...])
    o_ref[0] = ((yn + bonus) * gate).astype(BF16)


def _split_dot_left(w, x):
    hi = x.astype(BF16)
    lo = (x - hi.astype(F32)).astype(BF16)
    return _dot(w, hi) + _dot(w, lo)


def _rwkv(pbs, w0, wda, a0, wg, k_k, k_a, r_k, gn_g, gn_b, ones_bd, bdm, msl, mil, tri):
    bsz, tp, _ = pbs.shape
    tr = _row_tile(tp)
    consts = [w0, wda, a0, wg, k_k, k_a, r_k, gn_g, gn_b, ones_bd, bdm, msl, mil, tri]
    tok = lambda w: pl.BlockSpec((1, tr, w), lambda b, j: (b, j, 0))
    return pl.pallas_call(
        functools.partial(_rwkv_kernel, tr=tr),
        grid=(bsz, tp // tr),
        in_specs=[tok(B_COLS)] + [_const_spec(c.shape) for c in consts],
        out_specs=tok(B_WIDTH),
        out_shape=jax.ShapeDtypeStruct((bsz, tp, B_WIDTH), BF16),
        scratch_shapes=[pltpu.VMEM((B_WIDTH // GROUP, GROUP, GROUP), F32)]
        + [pltpu.VMEM((tr, B_WIDTH), F32)] * 7,
        compiler_params=pltpu.CompilerParams(
            dimension_semantics=("arbitrary", "arbitrary"), vmem_limit_bytes=VMEM_LIMIT),
        name="rwkv",
    )(pbs, *consts)


def _gelu_tanh(x):
    return 0.5 * x * (1.0 + jnp.tanh(0.7978845608028654 * (x + 0.044715 * (x * x * x))))


def _ffn_kernel(x_ref, ya_ref, yb_ref, lng_ref, lnb_ref, wout_ref, l1g_ref, l1b_ref, wup_ref,
                cw_ref, cb_ref, wdn_ref, l2g_ref, l2b_ref, o_ref, hist_ref, act_ref, *, tr):
    j = pl.program_id(1)

    @pl.when(j == 0)
    def _():
        hist_ref[...] = jnp.zeros_like(hist_ref)

    h0 = _layer_norm(x_ref[0], lng_ref[...], lnb_ref[...])
    mix = _dot(ya_ref[0], wout_ref[:A_WIDTH, :]) + _dot(yb_ref[0], wout_ref[A_WIDTH:, :])
    h1 = _layer_norm(ALPHA * h0 + mix, l1g_ref[...], l1b_ref[...])
    h1b = h1.astype(BF16)
    row = lax.broadcasted_iota(jnp.int32, (tr, 1), 0)
    valid = (j * tr + row) >= PAD

    def up(c):
        return (_dot(h1b, wup_ref[:, c * FF_CHUNK:(c + 1) * FF_CHUNK]),
                _dot(h1b, wup_ref[:, D_FF + c * FF_CHUNK:D_FF + (c + 1) * FF_CHUNK]))

    nxt = up(0)
    for c in range(N_FF_CHUNKS):
        cs = slice(c * FF_CHUNK, (c + 1) * FF_CHUNK)
        gate, val = nxt
        if c + 1 < N_FF_CHUNKS:
            nxt = up(c + 1)
        gate = jnp.where(valid, gate, 0.0)
        hist = hist_ref[c]
        g1 = jnp.where(row == 0, hist[7:8, :], pltpu.roll(gate, 1, 0))
        g2 = jnp.where(row == 0, hist[6:7, :], jnp.where(row == 1, hist[7:8, :], pltpu.roll(gate, 2, 0)))
        hist_ref[c] = gate[tr - 8:, :]
        conv = cw_ref[0:1, cs] * g2 + cw_ref[1:2, cs] * g1 + cw_ref[2:3, cs] * gate + cb_ref[:, cs]
        act_ref[:, cs] = (_gelu_tanh(conv) * val).astype(BF16)
    ffn = _dot(act_ref[...], wdn_ref[...])
    o_ref[0] = _layer_norm(ALPHA * h1 + ffn, l2g_ref[...], l2b_ref[...])


def _ffn(xcat, ya, yb, ln_g, ln_b, wout, l1g, l1b, wup, cw, cb, wdn, l2g, l2b):
    bsz, tp, _ = xcat.shape
    tr = _row_tile(tp)
    tok = lambda w: pl.BlockSpec((1, tr, w), lambda b, j: (b, j, 0))

    def res(c):
        nd = c.ndim
        return pl.BlockSpec(c.shape, lambda *_: (0,) * nd, pipeline_mode=pl.Buffered(1))

    return pl.pallas_call(
        functools.partial(_ffn_kernel, tr=tr),
        grid=(bsz, tp // tr),
        in_specs=[tok(D_MODEL), tok(A_WIDTH), tok(B_WIDTH), res(ln_g), res(ln_b), res(wout), res(l1g),
                  res(l1b), res(wup), res(cw), res(cb), res(wdn), res(l2g), res(l2b)],
        out_specs=tok(D_MODEL),
        out_shape=jax.ShapeDtypeStruct((bsz, tp, D_MODEL), F32),
        scratch_shapes=[pltpu.VMEM((N_FF_CHUNKS, 8, FF_CHUNK), F32), pltpu.VMEM((tr, D_FF), BF16)],
        compiler_params=pltpu.CompilerParams(
            dimension_semantics=("arbitrary", "arbitrary"), vmem_limit_bytes=VMEM_LIMIT),
        name="ffn",
    )(xcat, ya, yb, ln_g, ln_b, wout, l1g, l1b, wup, cw, cb, wdn, l2g, l2b)


def _row(v):
    return v.reshape(1, -1).astype(F32)


def kernel(x, meta, ln_in_g, ln_in_b, w_in, q_norm_g, w_uq, kv_norm_g, w_ukv, w_iq, idx_ln_g, idx_ln_b, mu_shift, w0, w_decay_up, a0, w_aaa_up, w_gate_up, k_k, k_a, r_k, gn_g, gn_b, w_out, ln1_g, ln1_b, w_ffn_up, conv_w, conv_b, w_ffn_down, ln2_g, ln2_b):
    bsz, seq, _ = x.shape
    assert seq % TQ == 0 and w_in.shape[0] == DEPTH
    t_real = N_META + seq
    topk = min(INDEX_TOPK, t_real // 4)

    xcat = jnp.concatenate([jnp.zeros((bsz, PAD, D_MODEL), x.dtype),
                            jnp.broadcast_to(meta[None].astype(x.dtype), (bsz, N_META, D_MODEL)), x], axis=1)

    wi = w_in[0]
    kidx_cols = wi[:, Q_RANK + KV_RANK:Q_RANK + KV_RANK + IDX_DIM]
    win = jnp.concatenate([
        wi[:, :Q_RANK + KV_RANK], kidx_cols, kidx_cols, wi[:, Q_RANK + KV_RANK + IDX_DIM:A_COLS],
        jnp.zeros((D_MODEL, A_PAD - 512 - IDX_HEADS), F32), wi[:, A_COLS:]], axis=1).astype(BF16)
    wukv = w_ukv[0].reshape(KV_RANK, A_HEADS, 2, A_HEAD_DIM)
    wuk = wukv[:, :, 0].reshape(KV_RANK, A_WIDTH).astype(BF16)
    wuvt = wukv[:, :, 1].reshape(KV_RANK, A_WIDTH).T.astype(BF16)
    ig2 = _row(jnp.concatenate([idx_ln_g[0], idx_ln_g[0]]))
    ib2 = _row(jnp.concatenate([idx_ln_b[0], idx_ln_b[0]]))

    q, qi, widx, k, vt, kx, pbs = _inproj(
        xcat, _row(ln_in_g), _row(ln_in_b), win, _row(q_norm_g[0]), w_uq[0].astype(BF16),
        w_iq[0].astype(BF16), _row(kv_norm_g[0]), wuk, wuvt, ig2, ib2, _row(mu_shift[0]))

    ya = _attention(q, qi, widx, kx, k, vt, topk)

    zl = jnp.zeros((DECAY_LORA, B_WIDTH), F32)
    wda = jnp.concatenate([jnp.concatenate([w_decay_up[0], zl], axis=1),
                           jnp.concatenate([zl, w_aaa_up[0]], axis=1)], axis=0).astype(BF16)
    ci = jnp.arange(B_WIDTH)
    ones_bd = (ci[:, None] // B_HEAD_DIM == ci[None, :] // B_HEAD_DIM).astype(BF16)
    gi = jnp.arange(GROUP)
    bdm = (gi[:, None] // B_HEAD_DIM == gi[None, :] // B_HEAD_DIM).astype(F32)
    ti = jnp.arange(CHUNK)
    msl = (ti[:, None] > (gi[None, :] % CHUNK)).astype(F32)
    mil = (ti[:, None] >= (gi[None, :] % CHUNK)).astype(F32)
    tri = (ti[:, None] >= ti[None, :]).astype(BF16)
    yb = _rwkv(pbs, _row(w0[0]), wda, _row(a0[0]), w_gate_up[0].astype(BF16), _row(k_k[0]), _row(k_a[0]),
               _row(r_k[0]), _row(gn_g[0]), _row(gn_b[0]), ones_bd, bdm, msl, mil, tri)

    out = _ffn(xcat, ya, yb, _row(ln_in_g), _row(ln_in_b), w_out[0].astype(BF16), _row(ln1_g[0]),
               _row(ln1_b[0]), w_ffn_up[0].astype(BF16), conv_w[0].astype(F32), _row(conv_b[0]),
               w_ffn_down[0].astype(BF16), _row(ln2_g[0]), _row(ln2_b[0]))
    return out[:, OFF:]
```
